```python
import jax, jax.numpy as jnp
from jax import lax
import numpy as np

D_MODEL = 1024
BATCH = 2
SEQ = 8192
DEPTH = 4
DEC_BATCH = 32
DEC_SEQ = 4
PAST_LEN = 8192
PAGE_SIZE = 128

N_BRANCH = 4
BR_WIDTH = 256
GM_GROUPS = 4
GM_CHUNK = 128
POOL_WINDOWS = (2, 4, 8, 16)
POOL_GROUP = BR_WIDTH // len(POOL_WINDOWS)
POOL_BUF = max(POOL_WINDOWS) - 1
N_HEADS_ATT = 4
HEAD_DIM = BR_WIDTH // N_HEADS_ATT
Q_BLOCK = 128
D_FF = 2816
CONV_W = 3
PLE_DIM = 256
FORGET_BIAS = 2.0
EPS = 1e-6

OFF_GM_U = N_BRANCH * D_MODEL
OFF_GM_V = OFF_GM_U + BR_WIDTH
OFF_POOL = OFF_GM_V + BR_WIDTH
OFF_FOX_Q = OFF_POOL + BR_WIDTH
OFF_FOX_K = OFF_FOX_Q + BR_WIDTH
OFF_FOX_V = OFF_FOX_K + BR_WIDTH
OFF_FOX_F = OFF_FOX_V + BR_WIDTH
OFF_SB_Q = OFF_FOX_F + N_HEADS_ATT
OFF_SB_K = OFF_SB_Q + BR_WIDTH
OFF_SB_V = OFF_SB_K + BR_WIDTH
N_IN = OFF_SB_V + BR_WIDTH

kernel_name = 'hybrid_gated_branch_decoder_step'


def _rmsnorm(x, g):
    xf = x.astype(jnp.float32)
    y = xf * lax.rsqrt(jnp.mean(xf * xf, axis=-1, keepdims=True) + EPS)
    return (y * g.astype(jnp.float32)).astype(x.dtype)


def _layernorm(x, g, b):
    xf = x.astype(jnp.float32)
    mu = jnp.mean(xf, axis=-1, keepdims=True)
    var = jnp.mean(jnp.square(xf - mu), axis=-1, keepdims=True)
    y = (xf - mu) * lax.rsqrt(var + EPS)
    return (y * g.astype(jnp.float32) + b.astype(jnp.float32)).astype(x.dtype)


def _gmlp_spatial_gate(u, v, ln_g, ln_b, w_s, b_s):
    n, t, c = v.shape
    vn = _layernorm(v, ln_g, ln_b)
    tp = -(-t // GM_CHUNK) * GM_CHUNK
    vc = jnp.pad(vn, ((0, 0), (0, tp - t), (0, 0))).reshape(n, tp // GM_CHUNK, GM_CHUNK, GM_GROUPS, c // GM_GROUPS)
    causal = jnp.tril(jnp.ones((GM_CHUNK, GM_CHUNK), dtype=bool))
    ws = jnp.where(causal[None], w_s, 0.0)
    s = jnp.einsum('gts,ncsgd->nctgd', ws, vc) + jnp.swapaxes(b_s, 0, 1)[None, None, :, :, None]
    s = s.reshape(n, tp, c)[:, :t]
    return u * s, vn


def _pool_mixer(xp, buf, pos0, w_pm, pm_scale):
    n, t, c = xp.shape
    full = jnp.concatenate([buf.astype(jnp.float32), xp.astype(jnp.float32)], axis=1)
    cs = jnp.concatenate([jnp.zeros((n, 1, c), jnp.float32), lax.cumsum(full, axis=1)], axis=1)
    pos = pos0 + jnp.arange(t)
    cur = full[:, POOL_BUF:]
    outs = []
    for g, w in enumerate(POOL_WINDOWS):
        sl = slice(g * POOL_GROUP, (g + 1) * POOL_GROUP)
        win = cs[:, POOL_BUF + 1:POOL_BUF + 1 + t, sl] - cs[:, POOL_BUF + 1 - w:POOL_BUF + 1 - w + t, sl]
        cnt = jnp.minimum(pos + 1, w).astype(jnp.float32)[None, :, None]
        outs.append(win / cnt - cur[:, :, sl])
    d = jnp.stack(outs, axis=2)
    y = jnp.einsum('ntgc,gce->ntge', d, w_pm.astype(jnp.float32)).reshape(n, t, c) * pm_scale.astype(jnp.float32)
    return y.astype(xp.dtype), full[:, -POOL_BUF:].astype(xp.dtype)


def _blocks(a, nb, qb):
    return jnp.swapaxes(a.reshape((a.shape[0], nb, qb) + a.shape[2:]), 0, 1)


def _forgetting_attention(q, k, v, cq, ck, qpos, kpos):
    n, tq, h, d = q.shape
    qb = Q_BLOCK if tq % Q_BLOCK == 0 else tq
    nb = tq // qb
    scale = d ** -0.5
    ckT = jnp.swapaxes(ck, 1, 2)

    def block(args):
        qi, ci, pi = args
        s = jnp.einsum('nqhd,nkhd->nhqk', qi, k).astype(jnp.float32) * scale
        s = s + jnp.swapaxes(ci, 1, 2)[..., None] - ckT[:, :, None, :]
        s = jnp.where(kpos[None, :] <= pi[:, None], s, -jnp.inf)
        p = jax.nn.softmax(s, axis=-1)
        return jnp.einsum('nhqk,nkhd->nqhd', p.astype(v.dtype), v)

    out = lax.map(block, (_blocks(q, nb, qb), _blocks(cq, nb, qb), qpos.reshape(nb, qb)))
    return jnp.swapaxes(out, 0, 1).reshape(n, tq, h, d)


def _stick_breaking_attention(q, k, v, qpos, kpos):
    n, tq, h, d = q.shape
    qb = Q_BLOCK if tq % Q_BLOCK == 0 else tq
    nb = tq // qb
    scale = d ** -0.5

    def block(args):
        qi, pi = args
        z = jnp.einsum('nqhd,nkhd->nhqk', qi, k).astype(jnp.float32) * scale
        mask = kpos[None, :] < pi[:, None]
        l_rest = jnp.where(mask, jax.nn.log_sigmoid(-z), 0.0)
        later = lax.cumsum(l_rest, axis=3, reverse=True) - l_rest
        a = jnp.where(mask, jnp.exp(jax.nn.log_sigmoid(z) + later), 0.0)
        return jnp.einsum('nhqk,nkhd->nqhd', a.astype(v.dtype), v)

    out = lax.map(block, (_blocks(q, nb, qb), qpos.reshape(nb, qb)))
    return jnp.swapaxes(out, 0, 1).reshape(n, tq, h, d)


def _conv_ffn(h, buf, w_up, conv_w, conv_b, w_down):
    t = h.shape[1]
    a = h @ w_up
    full = jnp.concatenate([buf.astype(a.dtype), a], axis=1)
    c = conv_b + sum(full[:, j:j + t] * conv_w[j] for j in range(CONV_W))
    gate, val = jnp.split(c, 2, axis=-1)
    return (jax.nn.gelu(gate) * val) @ w_down, full[:, -(CONV_W - 1):]


def _layer(x, pe, past_fk, past_fv, past_ff, past_sk, past_sv, pool_buf, ffn_buf,
           g_mix, w_in, b_f, gm_ln_g, gm_ln_b, gm_ws, gm_bs, pm_w, pm_scale, w_br, w_out,
           g_ffn, w_up, conv_w, conv_b, w_down, g_ple, w_ple_gate, w_ple_proj):
    n, t, _ = x.shape
    past = 0 if past_fk is None else past_fk.shape[1]
    h = _rmsnorm(x, g_mix)
    z = h @ w_in
    gates = jax.nn.sigmoid(z[..., :OFF_GM_U].reshape(n, t, N_BRANCH, D_MODEL))
    o_gm, gm_v = _gmlp_spatial_gate(z[..., OFF_GM_U:OFF_GM_V], z[..., OFF_GM_V:OFF_POOL], gm_ln_g, gm_ln_b, gm_ws, gm_bs)
    o_pool, pool_new = _pool_mixer(z[..., OFF_POOL:OFF_FOX_Q], pool_buf, past, pm_w, pm_scale)
    heads = lambda a: a.reshape(n, t, N_HEADS_ATT, HEAD_DIM)
    fq, fk, fv = heads(z[..., OFF_FOX_Q:OFF_FOX_K]), heads(z[..., OFF_FOX_K:OFF_FOX_V]), heads(z[..., OFF_FOX_V:OFF_FOX_F])
    flogf = jax.nn.log_sigmoid(z[..., OFF_FOX_F:OFF_SB_Q].astype(jnp.float32) + b_f.astype(jnp.float32))
    sq, sk, sv = heads(z[..., OFF_SB_Q:OFF_SB_K]), heads(z[..., OFF_SB_K:OFF_SB_V]), heads(z[..., OFF_SB_V:N_IN])
    if past_fk is None:
        kf, vf, lf, ksb, vsb = fk, fv, flogf, sk, sv
    else:
        kf = jnp.concatenate([past_fk, fk], axis=1)
        vf = jnp.concatenate([past_fv, fv], axis=1)
        lf = jnp.concatenate([past_ff.astype(jnp.float32), flogf], axis=1)
        ksb = jnp.concatenate([past_sk, sk], axis=1)
        vsb = jnp.concatenate([past_sv, sv], axis=1)
    cum = lax.cumsum(lf, axis=1)
    qpos = past + jnp.arange(t)
    kpos = jnp.arange(past + t)
    o_fox = _forgetting_attention(fq, kf, vf, cum[:, past:], cum, qpos, kpos).reshape(n, t, BR_WIDTH)
    o_sb = _stick_breaking_attention(sq, ksb, vsb, qpos, kpos).reshape(n, t, BR_WIDTH)
    o = jnp.stack([o_gm, o_pool, o_fox, o_sb], axis=2)
    merged = jnp.sum(gates * jnp.einsum('ntbc,bcd->ntbd', o, w_br), axis=2)
    x = x + merged @ w_out
    f, ffn_new = _conv_ffn(_rmsnorm(x, g_ffn), ffn_buf, w_up, conv_w, conv_b, w_down)
    x = x + f
    x = x + jax.nn.sigmoid(_rmsnorm(x, g_ple) @ w_ple_gate) * (pe @ w_ple_proj)
    return x, (fk, fv, flogf, sk, sv, pool_new, ffn_new, gm_v)


def setup_inputs(seed: int = 0) -> dict:
    key = jax.random.key(seed)
    ks = iter(jax.random.split(key, 48))
    nrm = lambda shape, scale=1.0: scale * jax.random.normal(next(ks), shape, jnp.float32)
    n_pages = PAST_LEN // PAGE_SIZE
    n_used = DEC_BATCH * n_pages
    n_pool = n_used + max(1, n_used // 4)
    page_table = jax.random.permutation(next(ks), n_pool)[:n_used].reshape(DEC_BATCH, n_pages).astype(jnp.int32)
    kv_shape = (DEPTH, n_pool, PAGE_SIZE, N_HEADS_ATT, HEAD_DIM)
    f2 = 2 * D_FF
    return {
        'x_prompt': nrm((BATCH, SEQ, D_MODEL)),
        'x_sample': nrm((DEC_BATCH, DEC_SEQ, D_MODEL)),
        'cache_fox_k': nrm(kv_shape),
        'cache_fox_v': nrm(kv_shape),
        'cache_fox_logf': jax.nn.log_sigmoid(FORGET_BIAS + nrm((DEPTH, n_pool, PAGE_SIZE, N_HEADS_ATT))),
        'cache_sb_k': nrm(kv_shape),
        'cache_sb_v': nrm(kv_shape),
        'state_pool': nrm((DEPTH, DEC_BATCH, POOL_BUF, BR_WIDTH)),
        'state_ffn_conv': nrm((DEPTH, DEC_BATCH, CONV_W - 1, f2)),
        'page_table': page_table,
        'p_prompt': nrm((DEPTH, BATCH, SEQ, PLE_DIM)),
        'p_sample': nrm((DEPTH, DEC_BATCH, DEC_SEQ, PLE_DIM)),
        'g_mix': 1.0 + nrm((DEPTH, D_MODEL), 0.05),
        'w_in': nrm((DEPTH, D_MODEL, N_IN), D_MODEL ** -0.5),
        'b_f': FORGET_BIAS + nrm((DEPTH, N_HEADS_ATT), 0.1),
        'gm_ln_g': 1.0 + nrm((DEPTH, BR_WIDTH), 0.05),
        'gm_ln_b': nrm((DEPTH, BR_WIDTH), 0.01),
        'gm_ws': nrm((DEPTH, GM_GROUPS, GM_CHUNK, GM_CHUNK), GM_CHUNK ** -0.5),
        'gm_bs': 1.0 + nrm((DEPTH, GM_GROUPS, GM_CHUNK), 0.1),
        'pm_w': nrm((DEPTH, len(POOL_WINDOWS), POOL_GROUP, POOL_GROUP), POOL_GROUP ** -0.5),
        'pm_scale': 1.0 + nrm((DEPTH, BR_WIDTH), 0.1),
        'w_br': nrm((DEPTH, N_BRANCH, BR_WIDTH, D_MODEL), BR_WIDTH ** -0.5),
        'w_out': nrm((DEPTH, D_MODEL, D_MODEL), 0.5 * D_MODEL ** -0.5),
        'g_ffn': 1.0 + nrm((DEPTH, D_MODEL), 0.05),
        'w_up': nrm((DEPTH, D_MODEL, f2), D_MODEL ** -0.5),
        'conv_w': nrm((DEPTH, CONV_W, f2), CONV_W ** -0.5),
        'conv_b': nrm((DEPTH, f2), 0.01),
        'w_down': nrm((DEPTH, D_FF, D_MODEL), D_FF ** -0.5),
        'g_ple': 1.0 + nrm((DEPTH, D_MODEL), 0.05),
        'w_ple_gate': nrm((DEPTH, D_MODEL, D_MODEL), D_MODEL ** -0.5),
        'w_ple_proj': nrm((DEPTH, PLE_DIM, D_MODEL), PLE_DIM ** -0.5),
        'g_final': 1.0 + nrm((D_MODEL,), 0.05),
    }


def reference(x_prompt, x_sample, cache_fox_k, cache_fox_v, cache_fox_logf, cache_sb_k, cache_sb_v,
              state_pool, state_ffn_conv, page_table, p_prompt, p_sample,
              g_mix, w_in, b_f, gm_ln_g, gm_ln_b, gm_ws, gm_bs, pm_w, pm_scale, w_br, w_out,
              g_ffn, w_up, conv_w, conv_b, w_down, g_ple, w_ple_gate, w_ple_proj, g_final):
    nbp, nbs = x_prompt.shape[0], x_sample.shape[0]
    past_len = page_table.shape[1] * cache_fox_k.shape[2]
    gather = lambda pool, i: pool[i][page_table].reshape((nbs, past_len) + pool.shape[3:])
    pool0 = jnp.zeros((nbp, POOL_BUF, BR_WIDTH), x_prompt.dtype)
    conv0 = jnp.zeros((nbp, CONV_W - 1, 2 * D_FF), x_prompt.dtype)
    xp, xs = x_prompt, x_sample
    new_p, new_s = [], []
    for i in range(DEPTH):
        wts = (g_mix[i], w_in[i], b_f[i], gm_ln_g[i], gm_ln_b[i], gm_ws[i], gm_bs[i], pm_w[i], pm_scale[i],
               w_br[i], w_out[i], g_ffn[i], w_up[i], conv_w[i], conv_b[i], w_down[i], g_ple[i],
               w_ple_gate[i], w_ple_proj[i])
        xp, st_p = _layer(xp, p_prompt[i], None, None, None, None, None, pool0, conv0, *wts)
        xs, st_s = _layer(xs, p_sample[i], gather(cache_fox_k, i), gather(cache_fox_v, i), gather(cache_fox_logf, i),
                          gather(cache_sb_k, i), gather(cache_sb_v, i), state_pool[i], state_ffn_conv[i], *wts)
        new_p.append(st_p)
        new_s.append(st_s)
    stk = lambda lst, j: jnp.stack([s[j] for s in lst], axis=0)
    y_prompt = _rmsnorm(xp, g_final)
    y_sample = _rmsnorm(xs, g_final)
    return (y_prompt, y_sample,
            stk(new_p, 0), stk(new_p, 1), stk(new_p, 2), stk(new_p, 3), stk(new_p, 4), stk(new_p, 5), stk(new_p, 6),
            stk(new_s, 0), stk(new_s, 1), stk(new_s, 2), stk(new_s, 3), stk(new_s, 4), stk(new_s, 5), stk(new_s, 6),
            stk(new_s, 7))
```

```python
import functools

import jax
import jax.numpy as jnp
from jax import lax
from jax.experimental import pallas as pl
from jax.experimental.pallas import tpu as pltpu

F32 = jnp.float32
BF = jnp.bfloat16

EPS = 1e-6
N_BRANCH = 4
BR_WIDTH = 256
N_HEADS = 4
HEAD_DIM = 64
LANES = 128
SUBLANES = 8
POOL_WINDOWS = (2, 4, 8, 16)
POOL_BUF = 15
POOL_HALO = 16
CONV_W = 3
GM_CHUNK = 128
NEG_BIG = -1e30
VMEM_LIMIT = 48 * 1024 * 1024

C_U, C_V, C_POOL, C_FQ, C_FK, C_FV, C_SQ, C_SK, C_SV, C_F = [BR_WIDTH * i for i in range(10)]
N_MIX = C_F + LANES


def _cparams(sem):
    return pltpu.CompilerParams(dimension_semantics=sem, vmem_limit_bytes=VMEM_LIMIT)


def _rms(x, g):
    return x * lax.rsqrt(jnp.mean(x * x, axis=-1, keepdims=True) + EPS) * g


def _softplus(x):
    return jnp.maximum(x, 0.0) + jnp.log1p(jnp.exp(-jnp.abs(x)))


def _split3(x):
    hi = x.astype(BF)
    r = x - hi.astype(F32)
    mid = r.astype(BF)
    r = r - mid.astype(F32)
    return hi, mid, r.astype(BF)


def _dot(a, b):
    return jnp.dot(a, b, preferred_element_type=F32)


def _dot_nt(a, b):
    return lax.dot_general(a, b, (((1,), (1,)), ((), ())), preferred_element_type=F32)


def _dot_exact(x, ones_bf, left):
    parts = _split3(x)
    if left:
        return _dot(ones_bf, parts[0]) + _dot(ones_bf, parts[1]) + _dot(ones_bf, parts[2])
    return _dot(parts[0], ones_bf) + _dot(parts[1], ones_bf) + _dot(parts[2], ones_bf)


def _const_spec(shape):
    nd = len(shape)
    return pl.BlockSpec(shape, lambda *_: (0,) * nd)


def _inproj_kernel(x_ref, g_ref, w_ref, wft_ref, bfr_ref, bfc_ref, lng_ref, lnb_ref, ws_ref, bs_ref,
                   ogm_ref, pool_ref, fq_ref, fk_ref, fv_ref, fkb_ref, fvb_ref, lf_ref, lft_ref, cc_ref, cr_ref,
                   sq_ref, sk_ref, sv_ref, skb_ref, svb_ref, vn_ref,
                   carc_ref, carr_ref, *, tb, seg, gm_rows, gm_chunk):
    i = pl.program_id(0)
    h = _rms(x_ref[...], g_ref[...]).astype(BF)
    z = _dot(h, w_ref[...])

    r = lax.broadcasted_iota(jnp.int32, (gm_rows, gm_rows), 0)
    c = lax.broadcasted_iota(jnp.int32, (gm_rows, gm_rows), 1)
    mix_mask = (r >= c) & ((r // gm_chunk) == (c // gm_chunk))
    grp = lax.broadcasted_iota(jnp.int32, (gm_rows, BR_WIDTH), 1) // HEAD_DIM
    for blk in range(tb // gm_rows):
        rows = slice(blk * gm_rows, (blk + 1) * gm_rows)
        u = z[rows, C_U:C_U + BR_WIDTH]
        v = z[rows, C_V:C_V + BR_WIDTH]
        mu = jnp.mean(v, axis=-1, keepdims=True)
        var = jnp.mean(jnp.square(v - mu), axis=-1, keepdims=True)
        vn = (v - mu) * lax.rsqrt(var + EPS) * lng_ref[...] + lnb_ref[...]
        vn_ref[rows, :] = vn
        vnb = vn.astype(BF)
        s = bs_ref[...]
        for g in range(N_HEADS):
            wg = jnp.where(mix_mask, ws_ref[g], 0.0).astype(BF)
            s = s + jnp.where(grp == g, _dot(wg, vnb), 0.0)
        ogm_ref[rows, :] = (u * s).astype(BF)

    pool_ref[...] = z[:, C_POOL:C_POOL + BR_WIDTH]
    scale = HEAD_DIM ** -0.5
    fq_ref[...] = (z[:, C_FQ:C_FQ + BR_WIDTH] * scale).astype(BF)
    fk = z[:, C_FK:C_FK + BR_WIDTH]
    fv = z[:, C_FV:C_FV + BR_WIDTH]
    fk_ref[...] = fk
    fv_ref[...] = fv
    fkb_ref[...] = fk.astype(BF)
    fvb_ref[...] = fv.astype(BF)
    sq_ref[...] = (z[:, C_SQ:C_SQ + BR_WIDTH] * scale).astype(BF)
    sk = z[:, C_SK:C_SK + BR_WIDTH]
    sv = z[:, C_SV:C_SV + BR_WIDTH]
    sk_ref[...] = sk
    sv_ref[...] = sv
    skb_ref[...] = sk.astype(BF)
    svb_ref[...] = sv.astype(BF)

    lf = -_softplus(-(z[:, C_F:C_F + LANES] + bfr_ref[...]))
    lft = -_softplus(-(_dot_nt(wft_ref[...], h) + bfc_ref[:, 0:1]))
    lf_ref[...] = lf
    lft_ref[...] = lft
    r = lax.broadcasted_iota(jnp.int32, (tb, tb), 0)
    c = lax.broadcasted_iota(jnp.int32, (tb, tb), 1)
    if seg >= tb:
        lower = (r >= c)
        upper = (r <= c)
        nblk_seq = seg // tb

        @pl.when(i % nblk_seq == 0)
        def _():
            carc_ref[...] = jnp.zeros_like(carc_ref)
            carr_ref[...] = jnp.zeros_like(carr_ref)
    else:
        same = (r // seg) == (c // seg)
        lower = (r >= c) & same
        upper = (r <= c) & same
        carc_ref[...] = jnp.zeros_like(carc_ref)
        carr_ref[...] = jnp.zeros_like(carr_ref)
    cum_c = _dot_exact(lf, jnp.where(lower, 1.0, 0.0).astype(BF), left=True) + carc_ref[0:1, :]
    cum_r = _dot_exact(lft, jnp.where(upper, 1.0, 0.0).astype(BF), left=False) + carr_ref[:, 0:1]
    cc_ref[...] = cum_c
    cr_ref[...] = cum_r
    carc_ref[...] = jnp.broadcast_to(cum_c[tb - 1:tb, :], carc_ref.shape)
    carr_ref[...] = jnp.broadcast_to(cum_r[:, tb - 1:tb], carr_ref.shape)


def _inproj(x, g_mix, w_mix, wft, bf_row, bf_col, ln_g, ln_b, ws, bs, *, tb, seg, gm_rows, gm_chunk):
    n, d = x.shape
    kern = functools.partial(_inproj_kernel, tb=tb, seg=seg, gm_rows=gm_rows, gm_chunk=gm_chunk)
    tok = lambda w: pl.BlockSpec((tb, w), lambda i: (i, 0))
    sds = jax.ShapeDtypeStruct
    out_shape = [
        sds((n, BR_WIDTH), BF),
        sds((n, BR_WIDTH), F32),
        sds((n, BR_WIDTH), BF),
        sds((n, BR_WIDTH), F32),
        sds((n, BR_WIDTH), F32),
        sds((n, BR_WIDTH), BF),
        sds((n, BR_WIDTH), BF),
        sds((n, LANES), F32),
        sds((SUBLANES, n), F32),
        sds((n, LANES), F32),
        sds((SUBLANES, n), F32),
        sds((n, BR_WIDTH), BF),
        sds((n, BR_WIDTH), F32),
        sds((n, BR_WIDTH), F32),
        sds((n, BR_WIDTH), BF),
        sds((n, BR_WIDTH), BF),
        sds((n, BR_WIDTH), F32),
    ]
    row = pl.BlockSpec((SUBLANES, tb), lambda i: (0, i))
    out_specs = [tok(BR_WIDTH)] * 7 + [tok(LANES), row, tok(LANES), row] + [tok(BR_WIDTH)] * 6
    return pl.pallas_call(
        kern,
        grid=(n // tb,),
        in_specs=[tok(d), _const_spec((1, d)), _const_spec(w_mix.shape), _const_spec(wft.shape),
                  _const_spec(bf_row.shape), _const_spec(bf_col.shape), _const_spec(ln_g.shape),
                  _const_spec(ln_b.shape), _const_spec(ws.shape), _const_spec(bs.shape)],
        out_specs=out_specs,
        out_shape=out_shape,
        scratch_shapes=[pltpu.VMEM((SUBLANES, LANES), F32), pltpu.VMEM((SUBLANES, LANES), F32)],
        compiler_params=_cparams(("arbitrary",)),
        name="inproj",
    )(x, g_mix, w_mix, wft, bf_row, bf_col, ln_g, ln_b, ws, bs)


def _pool_kernel(x_ref, w_ref, sc_ref, o_ref, ext_ref, *, tb, rows_per_seq, offset, pos0, nblk_seq):
    i = pl.program_id(0)

    @pl.when(i % nblk_seq == 0)
    def _():
        ext_ref[0:POOL_HALO, :] = jnp.zeros((POOL_HALO, BR_WIDTH), F32)

    ext_ref[POOL_HALO:POOL_HALO + tb, :] = x_ref[...]
    rowi = lax.broadcasted_iota(jnp.int32, (tb, LANES), 0)
    lane = lax.broadcasted_iota(jnp.int32, (tb, LANES), 1)
    pos = pos0 + (i * tb + rowi) % rows_per_seq - offset
    first = lane < HEAD_DIM
    halves = []
    for half in range(2):
        cols = slice(half * LANES, (half + 1) * LANES)
        shifted = lambda j: ext_ref[POOL_HALO - j:POOL_HALO - j + tb, cols]
        w_small, w_big = POOL_WINDOWS[2 * half], POOL_WINDOWS[2 * half + 1]
        cur = shifted(0)
        acc = cur
        for j in range(1, w_small):
            acc = acc + shifted(j)
        small = acc
        for j in range(w_small, w_big):
            acc = acc + shifted(j)
        win = jnp.where(first, small, acc)
        width = jnp.where(first, w_small, w_big)
        cnt = jnp.clip(pos + 1, 1, width).astype(F32)
        halves.append(win / cnt - cur)
    d = jnp.concatenate(halves, axis=1).astype(BF)
    o_ref[...] = (_dot(d, w_ref[...]) * sc_ref[...]).astype(BF)
    ext_ref[0:POOL_HALO, :] = ext_ref[tb:tb + POOL_HALO, :]


def _pool(x, w_bd, scale, *, tb, rows_per_seq, offset, pos0):
    n = x.shape[0]
    kern = functools.partial(_pool_kernel, tb=tb, rows_per_seq=rows_per_seq, offset=offset, pos0=pos0,
                             nblk_seq=max(1, rows_per_seq // tb))
    return pl.pallas_call(
        kern,
        grid=(n // tb,),
        in_specs=[pl.BlockSpec((tb, BR_WIDTH), lambda i: (i, 0)), _const_spec(w_bd.shape), _const_spec(scale.shape)],
        out_specs=pl.BlockSpec((tb, BR_WIDTH), lambda i: (i, 0)),
        out_shape=jax.ShapeDtypeStruct((n, BR_WIDTH), BF),
        scratch_shapes=[pltpu.VMEM((tb + POOL_HALO, BR_WIDTH), F32)],
        compiler_params=_cparams(("arbitrary",)),
        name="pool",
    )(x, w_bd, scale)


def _head_queries(q):
    left = lax.broadcasted_iota(jnp.int32, (q.shape[0], LANES), 1) < HEAD_DIM
    out = []
    for h in range(N_HEADS):
        pair = q[:, (h // 2) * LANES:(h // 2 + 1) * LANES]
        keep = left if h % 2 == 0 else jnp.logical_not(left)
        out.append(jnp.where(keep, pair, jnp.zeros_like(pair)))
    return out, left


def _fox_kernel(q_ref, k_ref, v_ref, cc_ref, cr_ref, o_ref, *, tq):
    qb = pl.program_id(1)
    qh, left = _head_queries(q_ref[...])
    cc = cc_ref[...]
    cq = [cc[:, h:h + 1] for h in range(N_HEADS)]
    r = lax.broadcasted_iota(jnp.int32, (tq, tq), 0)
    c = lax.broadcasted_iota(jnp.int32, (tq, tq), 1)
    causal = c <= r

    def block(j, carry, masked):
        ms, ls, accs = carry
        start = pl.multiple_of(j * tq, tq)
        kblk = k_ref[pl.ds(start, tq), :]
        vblk = v_ref[pl.ds(start, tq), :]
        new_m, new_l, pv, alphas = [], [], [], []
        for h in range(N_HEADS):
            pair = slice((h // 2) * LANES, (h // 2 + 1) * LANES)
            ck = cr_ref[h:h + 1, pl.ds(start, tq)]
            s = _dot_nt(qh[h], kblk[:, pair]) + (cq[h] - ck)
            if masked:
                s = jnp.where(causal, s, NEG_BIG)
            m_new = jnp.maximum(ms[h], jnp.max(s, axis=-1, keepdims=True))
            alpha = jnp.exp(ms[h] - m_new)
            p = jnp.exp(s - m_new)
            new_m.append(m_new)
            new_l.append(ls[h] * alpha + jnp.sum(p, axis=-1, keepdims=True))
            alphas.append(alpha)
            pv.append(_dot(p.astype(BF), vblk[:, pair]))
        new_acc = []
        for hp in range(N_HEADS // 2):
            a = jnp.where(left, alphas[2 * hp], alphas[2 * hp + 1])
            new_acc.append(accs[hp] * a + jnp.where(left, pv[2 * hp], pv[2 * hp + 1]))
        return tuple(new_m), tuple(new_l), tuple(new_acc)

    init = (tuple(jnp.full((tq, 1), NEG_BIG, F32) for _ in range(N_HEADS)),
            tuple(jnp.zeros((tq, 1), F32) for _ in range(N_HEADS)),
            tuple(jnp.zeros((tq, LANES), F32) for _ in range(N_HEADS // 2)))
    carry = lax.fori_loop(0, qb, lambda j, cr: block(j, cr, False), init)
    ms, ls, accs = block(qb, carry, True)
    for hp in range(N_HEADS // 2):
        den = jnp.where(left, ls[2 * hp], ls[2 * hp + 1])
        o_ref[:, hp * LANES:(hp + 1) * LANES] = (accs[hp] / den).astype(BF)


def _fox(q, kb, vb, cum_c, cum_r, *, nseq, t, tq):
    kern = functools.partial(_fox_kernel, tq=tq)
    nq = t // tq
    return pl.pallas_call(
        kern,
        grid=(nseq, nq),
        in_specs=[pl.BlockSpec((tq, BR_WIDTH), lambda n, i: (n * nq + i, 0)),
                  pl.BlockSpec((t, BR_WIDTH), lambda n, i: (n, 0)),
                  pl.BlockSpec((t, BR_WIDTH), lambda n, i: (n, 0)),
                  pl.BlockSpec((tq, LANES), lambda n, i: (n * nq + i, 0)),
                  pl.BlockSpec((SUBLANES, t), lambda n, i: (0, n))],
        out_specs=pl.BlockSpec((tq, BR_WIDTH), lambda n, i: (n * nq + i, 0)),
        out_shape=jax.ShapeDtypeStruct((nseq * t, BR_WIDTH), BF),
        compiler_params=_cparams(("arbitrary", "arbitrary")),
        name="fox_prompt",
    )(q, kb, vb, cum_c, cum_r)


def _sb_kernel(q_ref, k_ref, v_ref, o_ref, *, tq):
    qb = pl.program_id(1)
    qh, left = _head_queries(q_ref[...])
    r = lax.broadcasted_iota(jnp.int32, (tq, tq), 0)
    c = lax.broadcasted_iota(jnp.int32, (tq, tq), 1)
    strict = c < r
    after = jnp.where(r > c, 1.0, 0.0).astype(BF)

    def block(j, carry, masked):
        cs, accs = carry
        start = pl.multiple_of(j * tq, tq)
        kblk = k_ref[pl.ds(start, tq), :]
        vblk = v_ref[pl.ds(start, tq), :]
        new_c, av = [], []
        for h in range(N_HEADS):
            pair = slice((h // 2) * LANES, (h // 2 + 1) * LANES)
            z = _dot_nt(qh[h], kblk[:, pair])
            lrest = -_softplus(z)
            if masked:
                lrest = jnp.where(strict, lrest, 0.0)
            later = _dot_exact(lrest, after, left=False) + cs[h]
            a = jnp.exp(z + lrest + later)
            if masked:
                a = jnp.where(strict, a, 0.0)
            av.append(_dot(a.astype(BF), vblk[:, pair]))
            new_c.append(cs[h] + jnp.sum(lrest, axis=-1, keepdims=True))
        new_acc = tuple(accs[hp] + jnp.where(left, av[2 * hp], av[2 * hp + 1]) for hp in range(N_HEADS // 2))
        return tuple(new_c), new_acc

    init = (tuple(jnp.zeros((tq, 1), F32) for _ in range(N_HEADS)),
            tuple(jnp.zeros((tq, LANES), F32) for _ in range(N_HEADS // 2)))
    carry = block(qb, init, True)
    _, accs = lax.fori_loop(0, qb, lambda jj, cr: block(qb - 1 - jj, cr, False), carry)
    for hp in range(N_HEADS // 2):
        o_ref[:, hp * LANES:(hp + 1) * LANES] = accs[hp].astype(BF)


def _sb(q, kb, vb, *, nseq, t, tq):
    kern = functools.partial(_sb_kernel, tq=tq)
    nq = t // tq
    return pl.pallas_call(
        kern,
        grid=(nseq, nq),
        in_specs=[pl.BlockSpec((tq, BR_WIDTH), lambda n, i: (n * nq + i, 0)),
                  pl.BlockSpec((t, BR_WIDTH), lambda n, i: (n, 0)),
                  pl.BlockSpec((t, BR_WIDTH), lambda n, i: (n, 0))],
        out_specs=pl.BlockSpec((tq, BR_WIDTH), lambda n, i: (n * nq + i, 0)),
        out_shape=jax.ShapeDtypeStruct((nseq * t, BR_WIDTH), BF),
        compiler_params=_cparams(("arbitrary", "arbitrary")),
        name="sb_prompt",
    )(q, kb, vb)


def _dec_attn_kernel(pt_ref, qf_ref, qs_ref, knf_ref, vnf_ref, lfn_ref, kns_ref, vns_ref, *rest, npp, tnew):
    del pt_ref
    DEC_ROWS = tnew * SUBLANES
    pages = rest[:5 * npp]
    of_ref, os_ref = rest[5 * npp:5 * npp + 2]
    m_ref, l_ref, accf_ref, offf_ref, e_ref, cs_ref, accs_ref = rest[5 * npp + 2:]
    s_id = pl.program_id(1)
    n_steps = pl.num_programs(1)

    r = lax.broadcasted_iota(jnp.int32, (LANES, LANES), 0)
    c = lax.broadcasted_iota(jnp.int32, (LANES, LANES), 1)
    after = jnp.where(r > c, 1.0, 0.0).astype(BF)
    rowi = lax.broadcasted_iota(jnp.int32, (DEC_ROWS, LANES), 0)
    lane = lax.broadcasted_iota(jnp.int32, (DEC_ROWS, LANES), 1)
    t_row = rowi // SUBLANES

    def suffix(l_list):
        nb = len(l_list)
        stack = l_list[0] if nb == 1 else jnp.concatenate(l_list, axis=0)
        inner = _dot_exact(stack, after, left=False)
        out = [None] * nb
        off = jnp.zeros((DEC_ROWS, 1), F32)
        for b in range(nb - 1, -1, -1):
            out[b] = inner[b * DEC_ROWS:(b + 1) * DEC_ROWS] + off
            off = off + jnp.sum(l_list[b], axis=-1, keepdims=True)
        return out, off

    def process(kf, vf, lf8, ks, vs, is_new):
        nb = len(kf)
        qf = qf_ref[...]
        qs = qs_ref[...]
        lf = [jnp.concatenate([x] * (DEC_ROWS // SUBLANES), axis=0) for x in lf8]
        rsum, tot = suffix(lf)
        base = offf_ref[...] - e_ref[...]
        s_list = []
        for b in range(nb):
            s = _dot_nt(qf, kf[b].astype(BF)) + rsum[b] + base
            if is_new:
                s = jnp.where((lane <= t_row) & (lane < tnew), s, NEG_BIG)
            s_list.append(s)
        m_old = m_ref[...]
        m_blk = s_list[0]
        for b in range(1, nb):
            m_blk = jnp.maximum(m_blk, s_list[b])
        m_new = jnp.maximum(m_old, jnp.max(m_blk, axis=-1, keepdims=True))
        alpha = jnp.exp(m_old - m_new)
        psum = jnp.zeros((DEC_ROWS, LANES), F32)
        pv = jnp.zeros((DEC_ROWS, BR_WIDTH), F32)
        for b in range(nb):
            p = jnp.exp(s_list[b] - m_new)
            psum = psum + p
            pv = pv + _dot(p.astype(BF), vf[b].astype(BF))
        m_ref[...] = m_new
        l_ref[...] = l_ref[...] * alpha + jnp.sum(psum, axis=-1, keepdims=True)
        accf_ref[...] = accf_ref[...] * alpha[:, 0:1] + pv
        offf_ref[...] = offf_ref[...] + tot
        z_list, l_list = [], []
        for b in range(nb):
            z = _dot_nt(qs, ks[b].astype(BF))
            lrest = -_softplus(z)
            if is_new:
                lrest = jnp.where((lane < t_row) & (lane < tnew), lrest, 0.0)
            z_list.append(z)
            l_list.append(lrest)
        later, tot_s = suffix(l_list)
        cs = cs_ref[...]
        av = jnp.zeros((DEC_ROWS, BR_WIDTH), F32)
        for b in range(nb):
            a = jnp.exp(z_list[b] + l_list[b] + later[b] + cs)
            if is_new:
                a = jnp.where((lane < t_row) & (lane < tnew), a, 0.0)
            av = av + _dot(a.astype(BF), vs[b].astype(BF))
        accs_ref[...] = accs_ref[...] + av
        cs_ref[...] = cs + tot_s

    @pl.when(s_id == 0)
    def _():
        m_ref[...] = jnp.full(m_ref.shape, NEG_BIG, F32)
        l_ref[...] = jnp.zeros_like(l_ref)
        accf_ref[...] = jnp.zeros_like(accf_ref)
        offf_ref[...] = jnp.zeros_like(offf_ref)
        cs_ref[...] = jnp.zeros_like(cs_ref)
        accs_ref[...] = jnp.zeros_like(accs_ref)
        lfn = jnp.concatenate([lfn_ref[...]] * (DEC_ROWS // SUBLANES), axis=0)
        inner = _dot_exact(lfn, after, left=False)
        e = jnp.sum(jnp.where(lane == t_row, inner, 0.0), axis=-1, keepdims=True)
        e_ref[...] = jnp.broadcast_to(e, e_ref.shape)
        process([knf_ref[...]], [vnf_ref[...]], [lfn_ref[...]], [kns_ref[...]], [vns_ref[...]], True)

    @pl.when(s_id > 0)
    def _():
        get = lambda k: [pages[5 * j + k][...] for j in range(npp)]
        process(get(0), get(1), get(2), get(3), get(4), False)

    @pl.when(s_id == n_steps - 1)
    def _():
        own = (lax.broadcasted_iota(jnp.int32, (DEC_ROWS, BR_WIDTH), 1) // HEAD_DIM
               == lax.broadcasted_iota(jnp.int32, (DEC_ROWS, BR_WIDTH), 0) % SUBLANES)
        nt = DEC_ROWS // SUBLANES
        fo = jnp.where(own, accf_ref[...] / l_ref[:, 0:1], 0.0).reshape(nt, SUBLANES, BR_WIDTH)
        so = jnp.where(own, accs_ref[...], 0.0).reshape(nt, SUBLANES, BR_WIDTH)
        of_ref[...] = jnp.sum(fo, axis=1)
        os_ref[...] = jnp.sum(so, axis=1)


def _dec_attn(page_table, qf, qs, knf, vnf, lfn, kns, vns, ck_f, cv_f, clf, ck_s, cv_s, *, layer, npp, tnew):
    nb, n_pages = page_table.shape
    page = ck_f.shape[2]
    DEC_ROWS = tnew * SUBLANES
    n_steps = 1 + n_pages // npp
    kern = functools.partial(_dec_attn_kernel, npp=npp, tnew=tnew)
    per_b = lambda shape: pl.BlockSpec((None,) + shape, lambda b, s, pt: (b, 0, 0))

    def page_spec(j, rows, width):
        def idx(b, s, pt):
            return (layer, pt[b, n_pages - jnp.maximum(s, 1) * npp + j], 0, 0)
        return pl.BlockSpec((None, None, rows, width), idx)

    in_specs = [per_b((DEC_ROWS, BR_WIDTH)), per_b((DEC_ROWS, BR_WIDTH)),
                per_b((LANES, BR_WIDTH)), per_b((LANES, BR_WIDTH)), per_b((SUBLANES, LANES)),
                per_b((LANES, BR_WIDTH)), per_b((LANES, BR_WIDTH))]
    args = [qf, qs, knf, vnf, lfn, kns, vns]
    for j in range(npp):
        in_specs += [page_spec(j, page, BR_WIDTH), page_spec(j, page, BR_WIDTH), page_spec(j, SUBLANES, page),
                     page_spec(j, page, BR_WIDTH), page_spec(j, page, BR_WIDTH)]
        args += [ck_f, cv_f, clf, ck_s, cv_s]
    nt = DEC_ROWS // SUBLANES
    out_spec = pl.BlockSpec((None, nt, BR_WIDTH), lambda b, s, pt: (b, 0, 0))
    wide = pltpu.VMEM((DEC_ROWS, BR_WIDTH), F32)
    narrow = pltpu.VMEM((DEC_ROWS, LANES), F32)
    grid_spec = pltpu.PrefetchScalarGridSpec(
        num_scalar_prefetch=1,
        grid=(nb, n_steps),
        in_specs=in_specs,
        out_specs=[out_spec, out_spec],
        scratch_shapes=[narrow, narrow, wide, narrow, narrow, narrow, wide],
    )
    return pl.pallas_call(
        kern,
        grid_spec=grid_spec,
        out_shape=[jax.ShapeDtypeStruct((nb, nt, BR_WIDTH), F32)] * 2,
        compiler_params=_cparams(("arbitrary", "arbitrary")),
        name="dec_attn",
    )(page_table, *args)


def _merge_kernel(x_ref, og_ref, op_ref, of_ref, os_ref, g_ref, wg_ref, wb_ref, wo_ref, o_ref):
    x = x_ref[...]
    d = x.shape[1]
    h = _rms(x, g_ref[...]).astype(BF)
    acc = jnp.zeros(x.shape, F32)
    for b, ref in enumerate((og_ref, op_ref, of_ref, os_ref)):
        gate = jax.nn.sigmoid(_dot(h, wg_ref[:, b * d:(b + 1) * d]))
        acc = acc + gate * _dot(ref[...], wb_ref[b])
    o_ref[...] = x + _dot(acc.astype(BF), wo_ref[...])


def _resident(shape):
    nd = len(shape)
    return pl.BlockSpec(shape, lambda *_: (0,) * nd, pipeline_mode=pl.Buffered(1))


def _merge(x, o_gm, o_pool, o_fox, o_sb, g_mix, w_gate, w_br, w_out, *, tb):
    n, d = x.shape
    tok = lambda w: pl.BlockSpec((tb, w), lambda i: (i, 0))
    return pl.pallas_call(
        _merge_kernel,
        grid=(n // tb,),
        in_specs=[tok(d), tok(BR_WIDTH), tok(BR_WIDTH), tok(BR_WIDTH), tok(BR_WIDTH), _const_spec((1, d)),
                  _resident(w_gate.shape), _resident(w_br.shape), _resident(w_out.shape)],
        out_specs=tok(d),
        out_shape=jax.ShapeDtypeStruct((n, d), F32),
        compiler_params=_cparams(("arbitrary",)),
        name="merge",
    )(x, o_gm, o_pool, o_fox, o_sb, g_mix, w_gate, w_br, w_out)


CONV_HALO = 8


def _gelu_tanh(x):
    return 0.5 * x * (1.0 + jnp.tanh(0.7978845608028654 * (x + 0.044715 * (x * x * x))))


def _ffn_kernel(*refs, tb, seg, nblk_seq, final, with_state):
    (x_ref, pe_ref, gf_ref, wug_ref, wuv_ref, cwg_ref, cwv_ref, cbg_ref, cbv_ref, wd_ref,
     gp_ref, wpg_ref, wpp_ref, gfin_ref) = refs[:14]
    refs = refs[14:]
    if with_state:
        p1g_ref, p1v_ref, p2g_ref, p2v_ref = refs[:4]
        refs = refs[4:]
    o_ref, sg_ref, sv_ref, h_ref, acc_ref, cg_ref, cv_ref, ext_ref = refs
    i = pl.program_id(0)
    f = pl.program_id(1)
    nf = pl.num_programs(1)

    @pl.when(f == 0)
    def _():
        h_ref[...] = _rms(x_ref[...], gf_ref[...]).astype(BF)
        acc_ref[...] = jnp.zeros_like(acc_ref)

    @pl.when(i % nblk_seq == 0)
    def _():
        cg_ref[f] = jnp.zeros((CONV_HALO, cg_ref.shape[2]), F32)
        cv_ref[f] = jnp.zeros((CONV_HALO, cv_ref.shape[2]), F32)

    h = h_ref[...]
    tmod = lax.broadcasted_iota(jnp.int32, (tb, wug_ref.shape[1]), 0) % seg

    def conv(w_ref, cw_ref, cb_ref, carry_ref, state):
        a = _dot(h, w_ref[...])
        ext_ref[0:CONV_HALO, :] = carry_ref[f]
        ext_ref[CONV_HALO:CONV_HALO + tb, :] = a
        prev1 = ext_ref[CONV_HALO - 1:CONV_HALO - 1 + tb, :]
        prev2 = ext_ref[CONV_HALO - 2:CONV_HALO - 2 + tb, :]
        if with_state:
            prev1 = jnp.where(tmod < 1, state[0][...], prev1)
            prev2 = jnp.where(tmod < 2, state[1][...], prev2)
        carry_ref[f] = ext_ref[tb:tb + CONV_HALO, :]
        cw = cw_ref[...]
        return a, cb_ref[...] + prev2 * cw[0:1, :] + prev1 * cw[1:2, :] + a * cw[2:3, :]

    ag, cgate = conv(wug_ref, cwg_ref, cbg_ref, cg_ref, (p1g_ref, p2g_ref) if with_state else None)
    av, cval = conv(wuv_ref, cwv_ref, cbv_ref, cv_ref, (p1v_ref, p2v_ref) if with_state else None)
    if with_state:
        sg_ref[...] = ag
        sv_ref[...] = av
    else:
        sg_ref[...] = ag[tb - (CONV_W - 1):tb, :]
        sv_ref[...] = av[tb - (CONV_W - 1):tb, :]
    acc_ref[...] += _dot((_gelu_tanh(cgate) * cval).astype(BF), wd_ref[...])

    @pl.when(f == nf - 1)
    def _():
        x2 = x_ref[...] + acc_ref[...]
        gate = jax.nn.sigmoid(_dot(_rms(x2, gp_ref[...]).astype(BF), wpg_ref[...]))
        x3 = x2 + gate * _dot(pe_ref[...].astype(BF), wpp_ref[...])
        if final:
            x3 = _rms(x3, gfin_ref[...])
        o_ref[...] = x3


def _ffn(x, pe, g_ffn, w_up, conv_w, conv_b, w_down, g_ple, w_pg, w_pp, g_final, state, *, tb, seg, tf, final):
    n, d = x.shape
    dff = w_down.shape[0]
    nf = dff // tf
    with_state = state is not None
    nblk = n // tb
    nblk_seq = max(1, seg // tb)
    kern = functools.partial(_ffn_kernel, tb=tb, seg=seg, nblk_seq=nblk_seq, final=final, with_state=with_state)
    tok = lambda w: pl.BlockSpec((tb, w), lambda i, f: (i, 0))
    cst = lambda shape: pl.BlockSpec(shape, lambda i, f: (0,) * len(shape))
    in_specs = [tok(d), tok(pe.shape[1]), cst((1, d)),
                pl.BlockSpec((d, tf), lambda i, f: (0, f)), pl.BlockSpec((d, tf), lambda i, f: (0, nf + f)),
                pl.BlockSpec((CONV_W, tf), lambda i, f: (0, f)), pl.BlockSpec((CONV_W, tf), lambda i, f: (0, nf + f)),
                pl.BlockSpec((1, tf), lambda i, f: (0, f)), pl.BlockSpec((1, tf), lambda i, f: (0, nf + f)),
                pl.BlockSpec((tf, d), lambda i, f: (f, 0)),
                cst((1, d)), cst(w_pg.shape), cst(w_pp.shape), cst((1, d))]
    args = [x, pe, g_ffn, w_up, w_up, conv_w, conv_w, conv_b, conv_b, w_down, g_ple, w_pg, w_pp, g_final]
    if with_state:
        p1, p2 = state
        in_specs += [pl.BlockSpec((tb, tf), lambda i, f: (i, f)), pl.BlockSpec((tb, tf), lambda i, f: (i, nf + f)),
                     pl.BlockSpec((tb, tf), lambda i, f: (i, f)), pl.BlockSpec((tb, tf), lambda i, f: (i, nf + f))]
        args += [p1, p1, p2, p2]
        st_shape = jax.ShapeDtypeStruct((n, dff), F32)
        st_spec = pl.BlockSpec((tb, tf), lambda i, f: (i, f))
    else:
        st_shape = jax.ShapeDtypeStruct((nblk, CONV_W - 1, dff), F32)
        st_spec = pl.BlockSpec((None, CONV_W - 1, tf), lambda i, f: (i, 0, f))
    return pl.pallas_call(
        kern,
        grid=(nblk, nf),
        in_specs=in_specs,
        out_specs=[tok(d), st_spec, st_spec],
        out_shape=[jax.ShapeDtypeStruct((n, d), F32), st_shape, st_shape],
        scratch_shapes=[pltpu.VMEM((tb, d), BF), pltpu.VMEM((tb, d), F32),
                        pltpu.VMEM((nf, CONV_HALO, tf), F32), pltpu.VMEM((nf, CONV_HALO, tf), F32),
                        pltpu.VMEM((tb + CONV_HALO, tf), F32)],
        compiler_params=_cparams(("arbitrary", "arbitrary")),
        name="ffn",
    )(*args)


def _pick_block(n, target):
    b = min(n, target)
    while n % b:
        b //= 2
    return b


def kernel(x_prompt, x_sample, cache_fox_k, cache_fox_v, cache_fox_logf, cache_sb_k, cache_sb_v, state_pool, state_ffn_conv, page_table, p_prompt, p_sample, g_mix, w_in, b_f, gm_ln_g, gm_ln_b, gm_ws, gm_bs, pm_w, pm_scale, w_br, w_out, g_ffn, w_up, conv_w, conv_b, w_down, g_ple, w_ple_gate, w_ple_proj, g_final):
    nbp, t, d = x_prompt.shape
    nbs, ts, _ = x_sample.shape
    depth = w_in.shape[0]
    n_pool, page = cache_fox_k.shape[1], cache_fox_k.shape[2]
    n_pages = page_table.shape[1]
    past = n_pages * page
    dff = w_down.shape[1]
    f2 = 2 * dff
    off_mix = N_BRANCH * d
    off_f = off_mix + 6 * BR_WIDTH
    off_sbq = off_f + N_HEADS
    np_tok, ns_tok = nbp * t, nbs * ts

    tb_p = _pick_block(t, 512)
    tq = _pick_block(t, 256)
    tf = 256
    npp = _pick_block(n_pages, 8)

    row = lambda a: a.reshape(1, -1)
    xp = x_prompt.reshape(np_tok, d)
    xs = x_sample.reshape(ns_tok, d)
    kv_pages = lambda c: c.reshape(depth, n_pool, page, BR_WIDTH)
    ck_f, cv_f, ck_s, cv_s = kv_pages(cache_fox_k), kv_pages(cache_fox_v), kv_pages(cache_sb_k), kv_pages(cache_sb_v)
    clf = jnp.pad(jnp.swapaxes(cache_fox_logf, 2, 3), ((0, 0), (0, 0), (0, SUBLANES - N_HEADS), (0, 0)))

    gm_tile = lambda a: jnp.tile(a[:, :ts, :ts], (1, nbs, nbs))

    new_p = [[] for _ in range(7)]
    new_s = [[] for _ in range(8)]
    for i in range(depth):
        wi = w_in[i]
        w_mix = jnp.concatenate(
            [wi[:, off_mix:off_f], wi[:, off_sbq:], jnp.pad(wi[:, off_f:off_sbq], ((0, 0), (0, LANES - N_HEADS)))],
            axis=1).astype(BF)
        wft = jnp.pad(wi[:, off_f:off_sbq].T, ((0, SUBLANES - N_HEADS), (0, 0))).astype(BF)
        bf_row = jnp.pad(b_f[i], (0, LANES - N_HEADS)).reshape(1, LANES)
        bf_col = jnp.broadcast_to(jnp.pad(b_f[i], (0, SUBLANES - N_HEADS))[:, None], (SUBLANES, LANES))
        w_gate = wi[:, :off_mix].astype(BF)
        w_br_b = w_br[i].astype(BF)
        w_out_b = w_out[i].astype(BF)
        w_up_b = w_up[i].astype(BF)
        w_down_b = w_down[i].astype(BF)
        w_pg = w_ple_gate[i].astype(BF)
        w_pp = w_ple_proj[i].astype(BF)
        w_pm_bd = jax.scipy.linalg.block_diag(*[pm_w[i, g] for g in range(len(POOL_WINDOWS))]).astype(BF)
        bs_full = lambda rows: jnp.repeat(jnp.tile(gm_bs[i][:, :min(rows, GM_CHUNK)].T, (rows // min(rows, GM_CHUNK), 1)),
                                          HEAD_DIM, axis=1)
        common = (row(g_mix[i]), w_mix, wft, bf_row, bf_col, row(gm_ln_g[i]), row(gm_ln_b[i]))
        final = i == depth - 1

        gm_rows = min(GM_CHUNK, tb_p)
        (o_gm, pool_in, fq, fk, fv, fkb, fvb, lf, _, cum_c, cum_r, sq, sk, sv, skb, svb, _) = _inproj(
            xp, *common, gm_ws[i][:, :gm_rows, :gm_rows], bs_full(gm_rows),
            tb=tb_p, seg=t, gm_rows=gm_rows, gm_chunk=GM_CHUNK)
        o_pool = _pool(pool_in, w_pm_bd, row(pm_scale[i]), tb=tb_p, rows_per_seq=t, offset=0, pos0=0)
        o_fox = _fox(fq, fkb, fvb, cum_c, cum_r, nseq=nbp, t=t, tq=tq)
        o_sb = _sb(sq, skb, svb, nseq=nbp, t=t, tq=tq)
        x1 = _merge(xp, o_gm, o_pool, o_fox, o_sb, row(g_mix[i]), w_gate, w_br_b, w_out_b, tb=tb_p)
        xp, st_g, st_v = _ffn(x1, p_prompt[i].reshape(np_tok, -1), row(g_ffn[i]), w_up_b, conv_w[i], row(conv_b[i]),
                              w_down_b, row(g_ple[i]), w_pg, w_pp, row(g_final), None,
                              tb=tb_p, seg=t, tf=tf, final=final)
        heads = lambda a, n, tt: a.reshape(n, tt, N_HEADS, HEAD_DIM)
        new_p[0].append(heads(fk, nbp, t))
        new_p[1].append(heads(fv, nbp, t))
        new_p[2].append(lf[:, :N_HEADS].reshape(nbp, t, N_HEADS))
        new_p[3].append(heads(sk, nbp, t))
        new_p[4].append(heads(sv, nbp, t))
        new_p[5].append(pool_in.reshape(nbp, t, BR_WIDTH)[:, t - POOL_BUF:])
        last = slice(t // tb_p - 1, None, t // tb_p)
        new_p[6].append(jnp.concatenate([st_g[last], st_v[last]], axis=-1))

        (o_gm, pool_in, fq, fk, fv, _, _, lf, lft, _, _, sq, sk, sv, _, _, vn) = _inproj(
            xs, *common, gm_tile(gm_ws[i]), bs_full_sample(gm_bs[i], ts, nbs),
            tb=ns_tok, seg=ts, gm_rows=ns_tok, gm_chunk=ts)
        grp_rows = -(-(POOL_BUF + ts) // SUBLANES) * SUBLANES
        lead = grp_rows - POOL_BUF - ts
        full = jnp.concatenate([jnp.zeros((nbs, lead, BR_WIDTH), F32), state_pool[i],
                                pool_in.reshape(nbs, ts, BR_WIDTH)], axis=1)
        o_pool = _pool(full.reshape(nbs * grp_rows, BR_WIDTH), w_pm_bd, row(pm_scale[i]),
                       tb=nbs * grp_rows, rows_per_seq=grp_rows, offset=grp_rows - ts, pos0=past)
        o_pool = o_pool.reshape(nbs, grp_rows, BR_WIDTH)[:, grp_rows - ts:].reshape(ns_tok, BR_WIDTH)
        o_fox, o_sb = _dec_attn(
            page_table, _block_diag_queries(fq, nbs, ts), _block_diag_queries(sq, nbs, ts),
            _pad_new(fk, nbs, ts), _pad_new(fv, nbs, ts),
            jnp.pad(jnp.swapaxes(lft.reshape(SUBLANES, nbs, ts), 0, 1), ((0, 0), (0, 0), (0, LANES - ts))),
            _pad_new(sk, nbs, ts), _pad_new(sv, nbs, ts),
            ck_f, cv_f, clf, ck_s, cv_s, layer=i, npp=npp, tnew=ts)
        o_fox = o_fox.reshape(ns_tok, BR_WIDTH).astype(BF)
        o_sb = o_sb.reshape(ns_tok, BR_WIDTH).astype(BF)
        x1 = _merge(xs, o_gm, o_pool, o_fox, o_sb, row(g_mix[i]), w_gate, w_br_b, w_out_b, tb=ns_tok)
        buf = state_ffn_conv[i]
        zero = jnp.zeros((nbs, ts - 1, f2), F32)
        p1 = jnp.concatenate([buf[:, 1:2], zero], axis=1).reshape(ns_tok, f2)
        p2 = jnp.concatenate([buf, zero[:, 1:]], axis=1).reshape(ns_tok, f2)
        xs, a_g, a_v = _ffn(x1, p_sample[i].reshape(ns_tok, -1), row(g_ffn[i]), w_up_b, conv_w[i], row(conv_b[i]),
                            w_down_b, row(g_ple[i]), w_pg, w_pp, row(g_final), (p1, p2),
                            tb=ns_tok, seg=ts, tf=tf, final=final)
        new_s[0].append(heads(fk, nbs, ts))
        new_s[1].append(heads(fv, nbs, ts))
        new_s[2].append(lf[:, :N_HEADS].reshape(nbs, ts, N_HEADS))
        new_s[3].append(heads(sk, nbs, ts))
        new_s[4].append(heads(sv, nbs, ts))
        new_s[5].append(jnp.concatenate([state_pool[i], pool_in.reshape(nbs, ts, BR_WIDTH)], axis=1)[:, ts:])
        a_full = jnp.concatenate([a_g, a_v], axis=-1).reshape(nbs, ts, f2)
        new_s[6].append(jnp.concatenate([buf, a_full], axis=1)[:, ts:])
        new_s[7].append(vn.reshape(nbs, ts, BR_WIDTH))

    stk = lambda lst: jnp.stack(lst, axis=0)
    return (xp.reshape(nbp, t, d), xs.reshape(nbs, ts, d),
            *[stk(a) for a in new_p], *[stk(a) for a in new_s])


def bs_full_sample(gm_bs_i, ts, nbs):
    return jnp.repeat(jnp.tile(gm_bs_i[:, :ts].T, (nbs, 1)), HEAD_DIM, axis=1)


def _block_diag_queries(q, nbs, ts):
    q = q.reshape(nbs, ts, 1, N_HEADS, HEAD_DIM)
    eye = jnp.eye(SUBLANES, N_HEADS, dtype=q.dtype).reshape(1, 1, SUBLANES, N_HEADS, 1)
    return (q * eye).reshape(nbs, ts * SUBLANES, BR_WIDTH)


def _pad_new(a, nbs, ts):
    return jnp.pad(a.reshape(nbs, ts, BR_WIDTH), ((0, 0), (0, LANES - ts), (0, 0)))
```

```python
import functools

import jax
import jax.numpy as jnp
from jax import lax
from jax.experimental import pallas as pl
from jax.experimental.pallas import tpu as pltpu

F32 = jnp.float32
BF = jnp.bfloat16

EPS = 1e-6
N_BRANCH = 4
BR_WIDTH = 256
N_HEADS = 4
HEAD_DIM = 64
LANES = 128
SUBLANES = 8
POOL_WINDOWS = (2, 4, 8, 16)
POOL_BUF = 15
POOL_HALO = 16
CONV_W = 3
GM_CHUNK = 128
NEG_BIG = -1e30
EXP_ZERO_BELOW = -104.0
VMEM_LIMIT = 48 * 1024 * 1024

C_U, C_V, C_POOL, C_FQ, C_FK, C_FV, C_SQ, C_SK, C_SV, C_F = [BR_WIDTH * i for i in range(10)]
N_MIX = C_F + LANES


def _cparams(sem):
    return pltpu.CompilerParams(dimension_semantics=sem, vmem_limit_bytes=VMEM_LIMIT)


def _rms(x, g):
    return x * lax.rsqrt(jnp.mean(x * x, axis=-1, keepdims=True) + EPS) * g


def _softplus(x):
    return jnp.maximum(x, 0.0) + jnp.log1p(jnp.exp(-jnp.abs(x)))


def _split3(x):
    hi = x.astype(BF)
    r = x - hi.astype(F32)
    mid = r.astype(BF)
    r = r - mid.astype(F32)
    return hi, mid, r.astype(BF)


def _dot(a, b):
    return jnp.dot(a, b, preferred_element_type=F32)


def _dot_nt(a, b):
    return lax.dot_general(a, b, (((1,), (1,)), ((), ())), preferred_element_type=F32)


def _dot_exact(x, ones_bf, left):
    parts = _split3(x)
    if left:
        return _dot(ones_bf, parts[0]) + _dot(ones_bf, parts[1]) + _dot(ones_bf, parts[2])
    return _dot(parts[0], ones_bf) + _dot(parts[1], ones_bf) + _dot(parts[2], ones_bf)


def _const_spec(shape):
    nd = len(shape)
    return pl.BlockSpec(shape, lambda *_: (0,) * nd)


def _inproj_kernel(x_ref, g_ref, w_ref, bfr_ref, lng_ref, lnb_ref, ws_ref, bs_ref,
                   ogm_ref, pool_ref, fq_ref, fk_ref, fv_ref, fkb_ref, fvb_ref, lf_ref, lft_ref, cc_ref, cr_ref,
                   sq_ref, sk_ref, sv_ref, skb_ref, svb_ref, vn_ref,
                   carc_ref, *, tb, seg, gm_rows, gm_chunk):
    i = pl.program_id(0)
    h = _rms(x_ref[...], g_ref[...]).astype(BF)
    z = _dot(h, w_ref[...])

    r = lax.broadcasted_iota(jnp.int32, (gm_rows, gm_rows), 0)
    c = lax.broadcasted_iota(jnp.int32, (gm_rows, gm_rows), 1)
    mix_mask = (r >= c) & ((r // gm_chunk) == (c // gm_chunk))
    grp = lax.broadcasted_iota(jnp.int32, (gm_rows, BR_WIDTH), 1) // HEAD_DIM
    for blk in range(tb // gm_rows):
        rows = slice(blk * gm_rows, (blk + 1) * gm_rows)
        u = z[rows, C_U:C_U + BR_WIDTH]
        v = z[rows, C_V:C_V + BR_WIDTH]
        mu = jnp.mean(v, axis=-1, keepdims=True)
        var = jnp.mean(jnp.square(v - mu), axis=-1, keepdims=True)
        vn = (v - mu) * lax.rsqrt(var + EPS) * lng_ref[...] + lnb_ref[...]
        vn_ref[rows, :] = vn
        vnb = vn.astype(BF)
        s = bs_ref[...]
        for g in range(N_HEADS):
            wg = jnp.where(mix_mask, ws_ref[g], 0.0).astype(BF)
            s = s + jnp.where(grp == g, _dot(wg, vnb), 0.0)
        ogm_ref[rows, :] = (u * s).astype(BF)

    pool_ref[...] = z[:, C_POOL:C_POOL + BR_WIDTH]
    scale = HEAD_DIM ** -0.5
    fq_ref[...] = (z[:, C_FQ:C_FQ + BR_WIDTH] * scale).astype(BF)
    fk = z[:, C_FK:C_FK + BR_WIDTH]
    fv = z[:, C_FV:C_FV + BR_WIDTH]
    fk_ref[...] = fk
    fv_ref[...] = fv
    fkb_ref[...] = fk.astype(BF)
    fvb_ref[...] = fv.astype(BF)
    sq_ref[...] = (z[:, C_SQ:C_SQ + BR_WIDTH] * scale).astype(BF)
    sk = z[:, C_SK:C_SK + BR_WIDTH]
    sv = z[:, C_SV:C_SV + BR_WIDTH]
    sk_ref[...] = sk
    sv_ref[...] = sv
    skb_ref[...] = sk.astype(BF)
    svb_ref[...] = sv.astype(BF)

    lf = -_softplus(-(z[:, C_F:C_F + LANES] + bfr_ref[...]))
    lf_ref[...] = lf
    lft_ref[...] = lf.T[0:SUBLANES, :]
    r = lax.broadcasted_iota(jnp.int32, (tb, tb), 0)
    c = lax.broadcasted_iota(jnp.int32, (tb, tb), 1)
    if seg >= tb:
        lower = (r >= c)
        nblk_seq = seg // tb

        @pl.when(i % nblk_seq == 0)
        def _():
            carc_ref[...] = jnp.zeros_like(carc_ref)
    else:
        lower = (r >= c) & ((r // seg) == (c // seg))
        carc_ref[...] = jnp.zeros_like(carc_ref)
    cum_c = _dot_exact(lf, jnp.where(lower, 1.0, 0.0).astype(BF), left=True) + carc_ref[0:1, :]
    cc_ref[...] = cum_c
    cr_ref[...] = cum_c.T[0:SUBLANES, :]
    carc_ref[...] = jnp.broadcast_to(cum_c[tb - 1:tb, :], carc_ref.shape)


def _inproj(x, g_mix, w_mix, bf_row, ln_g, ln_b, ws, bs, *, tb, seg, gm_rows, gm_chunk):
    n, d = x.shape
    kern = functools.partial(_inproj_kernel, tb=tb, seg=seg, gm_rows=gm_rows, gm_chunk=gm_chunk)
    tok = lambda w: pl.BlockSpec((tb, w), lambda i: (i, 0))
    sds = jax.ShapeDtypeStruct
    out_shape = [
        sds((n, BR_WIDTH), BF),
        sds((n, BR_WIDTH), F32),
        sds((n, BR_WIDTH), BF),
        sds((n, BR_WIDTH), F32),
        sds((n, BR_WIDTH), F32),
        sds((n, BR_WIDTH), BF),
        sds((n, BR_WIDTH), BF),
        sds((n, LANES), F32),
        sds((SUBLANES, n), F32),
        sds((n, LANES), F32),
        sds((SUBLANES, n), F32),
        sds((n, BR_WIDTH), BF),
        sds((n, BR_WIDTH), F32),
        sds((n, BR_WIDTH), F32),
        sds((n, BR_WIDTH), BF),
        sds((n, BR_WIDTH), BF),
        sds((n, BR_WIDTH), F32),
    ]
    row = pl.BlockSpec((SUBLANES, tb), lambda i: (0, i))
    out_specs = [tok(BR_WIDTH)] * 7 + [tok(LANES), row, tok(LANES), row] + [tok(BR_WIDTH)] * 6
    return pl.pallas_call(
        kern,
        grid=(n // tb,),
        in_specs=[tok(d), _const_spec((1, d)), _const_spec(w_mix.shape),
                  _const_spec(bf_row.shape), _const_spec(ln_g.shape),
                  _const_spec(ln_b.shape), _const_spec(ws.shape), _const_spec(bs.shape)],
        out_specs=out_specs,
        out_shape=out_shape,
        scratch_shapes=[pltpu.VMEM((SUBLANES, LANES), F32)],
        compiler_params=_cparams(("arbitrary",)),
        name="inproj",
    )(x, g_mix, w_mix, bf_row, ln_g, ln_b, ws, bs)


def _pool_kernel(x_ref, w_ref, sc_ref, o_ref, ext_ref, *, tb, rows_per_seq, offset, pos0, nblk_seq):
    i = pl.program_id(0)

    @pl.when(i % nblk_seq == 0)
    def _():
        ext_ref[0:POOL_HALO, :] = jnp.zeros((POOL_HALO, BR_WIDTH), F32)

    ext_ref[POOL_HALO:POOL_HALO + tb, :] = x_ref[...]
    rowi = lax.broadcasted_iota(jnp.int32, (tb, LANES), 0)
    lane = lax.broadcasted_iota(jnp.int32, (tb, LANES), 1)
    pos = pos0 + (i * tb + rowi) % rows_per_seq - offset
    first = lane < HEAD_DIM
    halves = []
    for half in range(2):
        cols = slice(half * LANES, (half + 1) * LANES)
        shifted = lambda j: ext_ref[POOL_HALO - j:POOL_HALO - j + tb, cols]
        w_small, w_big = POOL_WINDOWS[2 * half], POOL_WINDOWS[2 * half + 1]
        cur = shifted(0)
        acc = cur
        for j in range(1, w_small):
            acc = acc + shifted(j)
        small = acc
        for j in range(w_small, w_big):
            acc = acc + shifted(j)
        win = jnp.where(first, small, acc)
        width = jnp.where(first, w_small, w_big)
        cnt = jnp.clip(pos + 1, 1, width).astype(F32)
        halves.append(win / cnt - cur)
    d = jnp.concatenate(halves, axis=1).astype(BF)
    o_ref[...] = (_dot(d, w_ref[...]) * sc_ref[...]).astype(BF)
    ext_ref[0:POOL_HALO, :] = ext_ref[tb:tb + POOL_HALO, :]


def _pool(x, w_bd, scale, *, tb, rows_per_seq, offset, pos0):
    n = x.shape[0]
    kern = functools.partial(_pool_kernel, tb=tb, rows_per_seq=rows_per_seq, offset=offset, pos0=pos0,
                             nblk_seq=max(1, rows_per_seq // tb))
    return pl.pallas_call(
        kern,
        grid=(n // tb,),
        in_specs=[pl.BlockSpec((tb, BR_WIDTH), lambda i: (i, 0)), _const_spec(w_bd.shape), _const_spec(scale.shape)],
        out_specs=pl.BlockSpec((tb, BR_WIDTH), lambda i: (i, 0)),
        out_shape=jax.ShapeDtypeStruct((n, BR_WIDTH), BF),
        scratch_shapes=[pltpu.VMEM((tb + POOL_HALO, BR_WIDTH), F32)],
        compiler_params=_cparams(("arbitrary",)),
        name="pool",
    )(x, w_bd, scale)


def _head_queries(q):
    left = lax.broadcasted_iota(jnp.int32, (q.shape[0], LANES), 1) < HEAD_DIM
    out = []
    for h in range(N_HEADS):
        pair = q[:, (h // 2) * LANES:(h // 2 + 1) * LANES]
        keep = left if h % 2 == 0 else jnp.logical_not(left)
        out.append(jnp.where(keep, pair, jnp.zeros_like(pair)))
    return out, left


def _fox_kernel(q_ref, k_ref, v_ref, cc_ref, cr_ref, o_ref, kn_ref, *, tq):
    qb = pl.program_id(1)
    qh, left = _head_queries(q_ref[...])
    cc = cc_ref[...]
    cq = [cc[:, h:h + 1] for h in range(N_HEADS)]
    r = lax.broadcasted_iota(jnp.int32, (tq, tq), 0)
    c = lax.broadcasted_iota(jnp.int32, (tq, tq), 1)
    causal = c <= r

    @pl.when(qb == 0)
    def _():
        hr = lax.broadcasted_iota(jnp.int32, (BR_WIDTH, LANES), 0) // HEAD_DIM
        hc = lax.broadcasted_iota(jnp.int32, (BR_WIDTH, LANES), 1)
        head_cols = jnp.where(hr == hc, 1.0, 0.0).astype(BF)

        def chunk(ci, best):
            kc = k_ref[pl.ds(pl.multiple_of(ci * tq, tq), tq), :].astype(F32)
            n2 = _dot_exact(kc * kc, head_cols, left=False)
            return jnp.maximum(best, jnp.max(n2, axis=0, keepdims=True))

        best = lax.fori_loop(0, k_ref.shape[0] // tq, chunk, jnp.zeros((1, LANES), F32))
        kn_ref[...] = jnp.broadcast_to(best, kn_ref.shape)

    reach = []
    for h in range(N_HEADS):
        qf = qh[h].astype(F32)
        qn2 = jnp.sum(qf * qf, axis=-1, keepdims=True)
        reach.append(jnp.sqrt(qn2 * kn_ref[0:1, h:h + 1]) * 1.001 + 0.01 + cq[h])

    def block(j, carry, masked):
        ms, ls, accs = carry
        start = pl.multiple_of(j * tq, tq)
        kblk = k_ref[pl.ds(start, tq), :]
        vblk = v_ref[pl.ds(start, tq), :]
        new_m, new_l, pv, alphas = [], [], [], []
        for h in range(N_HEADS):
            pair = slice((h // 2) * LANES, (h // 2 + 1) * LANES)
            ck = cr_ref[h:h + 1, pl.ds(start, tq)]
            s = _dot_nt(qh[h], kblk[:, pair]) + (cq[h] - ck)
            if masked:
                s = jnp.where(causal, s, NEG_BIG)
            m_new = jnp.maximum(ms[h], jnp.max(s, axis=-1, keepdims=True))
            alpha = jnp.exp(ms[h] - m_new)
            p = jnp.exp(s - m_new)
            new_m.append(m_new)
            new_l.append(ls[h] * alpha + jnp.sum(p, axis=-1, keepdims=True))
            alphas.append(alpha)
            pv.append(_dot(p.astype(BF), vblk[:, pair]))
        new_acc = []
        for hp in range(N_HEADS // 2):
            a = jnp.where(left, alphas[2 * hp], alphas[2 * hp + 1])
            new_acc.append(accs[hp] * a + jnp.where(left, pv[2 * hp], pv[2 * hp + 1]))
        return tuple(new_m), tuple(new_l), tuple(new_acc)

    init = (tuple(jnp.full((tq, 1), NEG_BIG, F32) for _ in range(N_HEADS)),
            tuple(jnp.zeros((tq, 1), F32) for _ in range(N_HEADS)),
            tuple(jnp.zeros((tq, LANES), F32) for _ in range(N_HEADS // 2)))
    def live(state):
        j, ms = state[0], state[1]
        start = pl.multiple_of(jnp.maximum(j, 0) * tq, tq)
        gap = None
        for h in range(N_HEADS):
            ck_last = cr_ref[h:h + 1, pl.ds(start, tq)][:, tq - 1:tq]
            g = reach[h] - ck_last - ms[h]
            gap = g if gap is None else jnp.maximum(gap, g)
        return (j >= 0) & (jnp.max(gap) > EXP_ZERO_BELOW)

    def step(state):
        j = state[0]
        return (j - 1,) + block(j, state[1:], False)

    first = block(qb, init, True)
    _, ms, ls, accs = lax.while_loop(live, step, (qb - 1,) + first)
    for hp in range(N_HEADS // 2):
        den = jnp.where(left, ls[2 * hp], ls[2 * hp + 1])
        o_ref[:, hp * LANES:(hp + 1) * LANES] = (accs[hp] / den).astype(BF)


def _fox(q, kb, vb, cum_c, cum_r, *, nseq, t, tq):
    kern = functools.partial(_fox_kernel, tq=tq)
    nq = t // tq
    return pl.pallas_call(
        kern,
        grid=(nseq, nq),
        in_specs=[pl.BlockSpec((tq, BR_WIDTH), lambda n, i: (n * nq + i, 0)),
                  pl.BlockSpec((t, BR_WIDTH), lambda n, i: (n, 0)),
                  pl.BlockSpec((t, BR_WIDTH), lambda n, i: (n, 0)),
                  pl.BlockSpec((tq, LANES), lambda n, i: (n * nq + i, 0)),
                  pl.BlockSpec((SUBLANES, t), lambda n, i: (0, n))],
        out_specs=pl.BlockSpec((tq, BR_WIDTH), lambda n, i: (n * nq + i, 0)),
        out_shape=jax.ShapeDtypeStruct((nseq * t, BR_WIDTH), BF),
        scratch_shapes=[pltpu.VMEM((SUBLANES, LANES), F32)],
        compiler_params=_cparams(("arbitrary", "arbitrary")),
        name="fox_prompt",
    )(q, kb, vb, cum_c, cum_r)


def _sb_kernel(q_ref, k_ref, v_ref, o_ref, *, tq):
    qb = pl.program_id(1)
    qh, left = _head_queries(q_ref[...])
    r = lax.broadcasted_iota(jnp.int32, (tq, tq), 0)
    c = lax.broadcasted_iota(jnp.int32, (tq, tq), 1)
    strict = c < r
    after = jnp.where(r > c, 1.0, 0.0).astype(BF)

    def block(j, carry, masked):
        cs, accs = carry
        start = pl.multiple_of(j * tq, tq)
        kblk = k_ref[pl.ds(start, tq), :]
        vblk = v_ref[pl.ds(start, tq), :]
        new_c, av = [], []
        for h in range(N_HEADS):
            pair = slice((h // 2) * LANES, (h // 2 + 1) * LANES)
            z = _dot_nt(qh[h], kblk[:, pair])
            lrest = -_softplus(z)
            if masked:
                lrest = jnp.where(strict, lrest, 0.0)
            later = _dot_exact(lrest, after, left=False) + cs[h]
            a = jnp.exp(z + lrest + later)
            if masked:
                a = jnp.where(strict, a, 0.0)
            av.append(_dot(a.astype(BF), vblk[:, pair]))
            new_c.append(cs[h] + jnp.sum(lrest, axis=-1, keepdims=True))
        new_acc = tuple(accs[hp] + jnp.where(left, av[2 * hp], av[2 * hp + 1]) for hp in range(N_HEADS // 2))
        return tuple(new_c), new_acc

    init = (tuple(jnp.zeros((tq, 1), F32) for _ in range(N_HEADS)),
            tuple(jnp.zeros((tq, LANES), F32) for _ in range(N_HEADS // 2)))
    def live(state):
        j, cs = state[0], state[1]
        top = cs[0]
        for h in range(1, N_HEADS):
            top = jnp.maximum(top, cs[h])
        return (j >= 0) & (jnp.max(top) > EXP_ZERO_BELOW)

    def step(state):
        j = state[0]
        return (j - 1,) + block(j, state[1:], False)

    first = block(qb, init, True)
    _, _, accs = lax.while_loop(live, step, (qb - 1,) + first)
    for hp in range(N_HEADS // 2):
        o_ref[:, hp * LANES:(hp + 1) * LANES] = accs[hp].astype(BF)


def _sb(q, kb, vb, *, nseq, t, tq):
    kern = functools.partial(_sb_kernel, tq=tq)
    nq = t // tq
    return pl.pallas_call(
        kern,
        grid=(nseq, nq),
        in_specs=[pl.BlockSpec((tq, BR_WIDTH), lambda n, i: (n * nq + i, 0)),
                  pl.BlockSpec((t, BR_WIDTH), lambda n, i: (n, 0)),
                  pl.BlockSpec((t, BR_WIDTH), lambda n, i: (n, 0))],
        out_specs=pl.BlockSpec((tq, BR_WIDTH), lambda n, i: (n * nq + i, 0)),
        out_shape=jax.ShapeDtypeStruct((nseq * t, BR_WIDTH), BF),
        compiler_params=_cparams(("arbitrary", "arbitrary")),
        name="sb_prompt",
    )(q, kb, vb)


def _dec_attn_kernel(pt_ref, qf_ref, qs_ref, knf_ref, vnf_ref, lfn_ref, kns_ref, vns_ref, *rest, npp, tnew):
    del pt_ref
    DEC_ROWS = tnew * SUBLANES
    pages = rest[:5 * npp]
    of_ref, os_ref = rest[5 * npp:5 * npp + 2]
    m_ref, l_ref, accf_ref, offf_ref, e_ref, cs_ref, accs_ref = rest[5 * npp + 2:]
    s_id = pl.program_id(1)
    n_steps = pl.num_programs(1)

    r = lax.broadcasted_iota(jnp.int32, (LANES, LANES), 0)
    c = lax.broadcasted_iota(jnp.int32, (LANES, LANES), 1)
    after = jnp.where(r > c, 1.0, 0.0).astype(BF)
    rowi = lax.broadcasted_iota(jnp.int32, (DEC_ROWS, LANES), 0)
    lane = lax.broadcasted_iota(jnp.int32, (DEC_ROWS, LANES), 1)
    t_row = rowi // SUBLANES

    def suffix(l_list):
        nb = len(l_list)
        stack = l_list[0] if nb == 1 else jnp.concatenate(l_list, axis=0)
        inner = _dot_exact(stack, after, left=False)
        out = [None] * nb
        off = jnp.zeros((DEC_ROWS, 1), F32)
        for b in range(nb - 1, -1, -1):
            out[b] = inner[b * DEC_ROWS:(b + 1) * DEC_ROWS] + off
            off = off + jnp.sum(l_list[b], axis=-1, keepdims=True)
        return out, off

    def process(kf, vf, lf8, ks, vs, is_new):
        nb = len(kf)
        qf = qf_ref[...]
        qs = qs_ref[...]
        lf = [jnp.concatenate([x] * (DEC_ROWS // SUBLANES), axis=0) for x in lf8]
        rsum, tot = suffix(lf)
        base = offf_ref[...] - e_ref[...]
        s_list = []
        for b in range(nb):
            s = _dot_nt(qf, kf[b].astype(BF)) + rsum[b] + base
            if is_new:
                s = jnp.where((lane <= t_row) & (lane < tnew), s, NEG_BIG)
            s_list.append(s)
        m_old = m_ref[...]
        m_blk = s_list[0]
        for b in range(1, nb):
            m_blk = jnp.maximum(m_blk, s_list[b])
        m_new = jnp.maximum(m_old, jnp.max(m_blk, axis=-1, keepdims=True))
        alpha = jnp.exp(m_old - m_new)
        psum = jnp.zeros((DEC_ROWS, LANES), F32)
        pv = jnp.zeros((DEC_ROWS, BR_WIDTH), F32)
        for b in range(nb):
            p = jnp.exp(s_list[b] - m_new)
            psum = psum + p
            pv = pv + _dot(p.astype(BF), vf[b].astype(BF))
        m_ref[...] = m_new
        l_ref[...] = l_ref[...] * alpha + jnp.sum(psum, axis=-1, keepdims=True)
        accf_ref[...] = accf_ref[...] * alpha[:, 0:1] + pv
        offf_ref[...] = offf_ref[...] + tot
        z_list, l_list = [], []
        for b in range(nb):
            z = _dot_nt(qs, ks[b].astype(BF))
            lrest = -_softplus(z)
            if is_new:
                lrest = jnp.where((lane < t_row) & (lane < tnew), lrest, 0.0)
            z_list.append(z)
            l_list.append(lrest)
        later, tot_s = suffix(l_list)
        cs = cs_ref[...]
        av = jnp.zeros((DEC_ROWS, BR_WIDTH), F32)
        for b in range(nb):
            a = jnp.exp(z_list[b] + l_list[b] + later[b] + cs)
            if is_new:
                a = jnp.where((lane < t_row) & (lane < tnew), a, 0.0)
            av = av + _dot(a.astype(BF), vs[b].astype(BF))
        accs_ref[...] = accs_ref[...] + av
        cs_ref[...] = cs + tot_s

    @pl.when(s_id == 0)
    def _():
        m_ref[...] = jnp.full(m_ref.shape, NEG_BIG, F32)
        l_ref[...] = jnp.zeros_like(l_ref)
        accf_ref[...] = jnp.zeros_like(accf_ref)
        offf_ref[...] = jnp.zeros_like(offf_ref)
        cs_ref[...] = jnp.zeros_like(cs_ref)
        accs_ref[...] = jnp.zeros_like(accs_ref)
        lfn = jnp.concatenate([lfn_ref[...]] * (DEC_ROWS // SUBLANES), axis=0)
        inner = _dot_exact(lfn, after, left=False)
        e = jnp.sum(jnp.where(lane == t_row, inner, 0.0), axis=-1, keepdims=True)
        e_ref[...] = jnp.broadcast_to(e, e_ref.shape)
        process([knf_ref[...]], [vnf_ref[...]], [lfn_ref[...]], [kns_ref[...]], [vns_ref[...]], True)

    @pl.when(s_id > 0)
    def _():
        get = lambda k: [pages[5 * j + k][...] for j in range(npp)]
        process(get(0), get(1), get(2), get(3), get(4), False)

    @pl.when(s_id == n_steps - 1)
    def _():
        own = (lax.broadcasted_iota(jnp.int32, (DEC_ROWS, BR_WIDTH), 1) // HEAD_DIM
               == lax.broadcasted_iota(jnp.int32, (DEC_ROWS, BR_WIDTH), 0) % SUBLANES)
        nt = DEC_ROWS // SUBLANES
        fo = jnp.where(own, accf_ref[...] / l_ref[:, 0:1], 0.0).reshape(nt, SUBLANES, BR_WIDTH)
        so = jnp.where(own, accs_ref[...], 0.0).reshape(nt, SUBLANES, BR_WIDTH)
        of_ref[...] = jnp.sum(fo, axis=1)
        os_ref[...] = jnp.sum(so, axis=1)


def _dec_attn(page_table, qf, qs, knf, vnf, lfn, kns, vns, ck_f, cv_f, clf, ck_s, cv_s, *, layer, npp, tnew):
    nb, n_pages = page_table.shape
    page = ck_f.shape[2]
    DEC_ROWS = tnew * SUBLANES
    n_steps = 1 + n_pages // npp
    kern = functools.partial(_dec_attn_kernel, npp=npp, tnew=tnew)
    per_b = lambda shape: pl.BlockSpec((None,) + shape, lambda b, s, pt: (b, 0, 0))

    def page_spec(j, rows, width):
        def idx(b, s, pt):
            return (layer, pt[b, n_pages - jnp.maximum(s, 1) * npp + j], 0, 0)
        return pl.BlockSpec((None, None, rows, width), idx)

    in_specs = [per_b((DEC_ROWS, BR_WIDTH)), per_b((DEC_ROWS, BR_WIDTH)),
                per_b((LANES, BR_WIDTH)), per_b((LANES, BR_WIDTH)), per_b((SUBLANES, LANES)),
                per_b((LANES, BR_WIDTH)), per_b((LANES, BR_WIDTH))]
    args = [qf, qs, knf, vnf, lfn, kns, vns]
    for j in range(npp):
        in_specs += [page_spec(j, page, BR_WIDTH), page_spec(j, page, BR_WIDTH), page_spec(j, SUBLANES, page),
                     page_spec(j, page, BR_WIDTH), page_spec(j, page, BR_WIDTH)]
        args += [ck_f, cv_f, clf, ck_s, cv_s]
    nt = DEC_ROWS // SUBLANES
    out_spec = pl.BlockSpec((None, nt, BR_WIDTH), lambda b, s, pt: (b, 0, 0))
    wide = pltpu.VMEM((DEC_ROWS, BR_WIDTH), F32)
    narrow = pltpu.VMEM((DEC_ROWS, LANES), F32)
    grid_spec = pltpu.PrefetchScalarGridSpec(
        num_scalar_prefetch=1,
        grid=(nb, n_steps),
        in_specs=in_specs,
        out_specs=[out_spec, out_spec],
        scratch_shapes=[narrow, narrow, wide, narrow, narrow, narrow, wide],
    )
    return pl.pallas_call(
        kern,
        grid_spec=grid_spec,
        out_shape=[jax.ShapeDtypeStruct((nb, nt, BR_WIDTH), F32)] * 2,
        compiler_params=_cparams(("arbitrary", "arbitrary")),
        name="dec_attn",
    )(page_table, *args)


def _merge_kernel(x_ref, og_ref, op_ref, of_ref, os_ref, g_ref, wg_ref, wb_ref, wo_ref, o_ref):
    x = x_ref[...]
    d = x.shape[1]
    h = _rms(x, g_ref[...]).astype(BF)
    acc = jnp.zeros(x.shape, F32)
    for b, ref in enumerate((og_ref, op_ref, of_ref, os_ref)):
        gate = jax.nn.sigmoid(_dot(h, wg_ref[:, b * d:(b + 1) * d]))
        acc = acc + gate * _dot(ref[...], wb_ref[b])
    o_ref[...] = x + _dot(acc.astype(BF), wo_ref[...])


def _resident(shape):
    nd = len(shape)
    return pl.BlockSpec(shape, lambda *_: (0,) * nd, pipeline_mode=pl.Buffered(1))


def _merge(x, o_gm, o_pool, o_fox, o_sb, g_mix, w_gate, w_br, w_out, *, tb):
    n, d = x.shape
    tok = lambda w: pl.BlockSpec((tb, w), lambda i: (i, 0))
    return pl.pallas_call(
        _merge_kernel,
        grid=(n // tb,),
        in_specs=[tok(d), tok(BR_WIDTH), tok(BR_WIDTH), tok(BR_WIDTH), tok(BR_WIDTH), _const_spec((1, d)),
                  _resident(w_gate.shape), _resident(w_br.shape), _resident(w_out.shape)],
        out_specs=tok(d),
        out_shape=jax.ShapeDtypeStruct((n, d), F32),
        compiler_params=_cparams(("arbitrary",)),
        name="merge",
    )(x, o_gm, o_pool, o_fox, o_sb, g_mix, w_gate, w_br, w_out)


CONV_HALO = 8


def _gelu_tanh(x):
    return 0.5 * x * (1.0 + jnp.tanh(0.7978845608028654 * (x + 0.044715 * (x * x * x))))


def _ffn_kernel(*refs, tb, seg, nblk_seq, final, with_state):
    (x_ref, pe_ref, gf_ref, wug_ref, wuv_ref, cwg_ref, cwv_ref, cbg_ref, cbv_ref, wd_ref,
     gp_ref, wpg_ref, wpp_ref, gfin_ref) = refs[:14]
    refs = refs[14:]
    if with_state:
        p1g_ref, p1v_ref, p2g_ref, p2v_ref = refs[:4]
        refs = refs[4:]
    o_ref, sg_ref, sv_ref, h_ref, acc_ref, cg_ref, cv_ref, ext_ref = refs
    i = pl.program_id(0)
    f = pl.program_id(1)
    nf = pl.num_programs(1)

    @pl.when(f == 0)
    def _():
        h_ref[...] = _rms(x_ref[...], gf_ref[...]).astype(BF)
        acc_ref[...] = jnp.zeros_like(acc_ref)

    @pl.when(i % nblk_seq == 0)
    def _():
        cg_ref[f] = jnp.zeros((CONV_HALO, cg_ref.shape[2]), F32)
        cv_ref[f] = jnp.zeros((CONV_HALO, cv_ref.shape[2]), F32)

    h = h_ref[...]
    tmod = lax.broadcasted_iota(jnp.int32, (tb, wug_ref.shape[1]), 0) % seg

    def conv(w_ref, cw_ref, cb_ref, carry_ref, state):
        a = _dot(h, w_ref[...])
        ext_ref[0:CONV_HALO, :] = carry_ref[f]
        ext_ref[CONV_HALO:CONV_HALO + tb, :] = a
        prev1 = ext_ref[CONV_HALO - 1:CONV_HALO - 1 + tb, :]
        prev2 = ext_ref[CONV_HALO - 2:CONV_HALO - 2 + tb, :]
        if with_state:
            prev1 = jnp.where(tmod < 1, state[0][...], prev1)
            prev2 = jnp.where(tmod < 2, state[1][...], prev2)
        carry_ref[f] = ext_ref[tb:tb + CONV_HALO, :]
        cw = cw_ref[...]
        return a, cb_ref[...] + prev2 * cw[0:1, :] + prev1 * cw[1:2, :] + a * cw[2:3, :]

    ag, cgate = conv(wug_ref, cwg_ref, cbg_ref, cg_ref, (p1g_ref, p2g_ref) if with_state else None)
    av, cval = conv(wuv_ref, cwv_ref, cbv_ref, cv_ref, (p1v_ref, p2v_ref) if with_state else None)
    if with_state:
        sg_ref[...] = ag
        sv_ref[...] = av
    else:
        sg_ref[...] = ag[tb - (CONV_W - 1):tb, :]
        sv_ref[...] = av[tb - (CONV_W - 1):tb, :]
    acc_ref[...] += _dot((_gelu_tanh(cgate) * cval).astype(BF), wd_ref[...])

    @pl.when(f == nf - 1)
    def _():
        x2 = x_ref[...] + acc_ref[...]
        gate = jax.nn.sigmoid(_dot(_rms(x2, gp_ref[...]).astype(BF), wpg_ref[...]))
        x3 = x2 + gate * _dot(pe_ref[...].astype(BF), wpp_ref[...])
        if final:
            x3 = _rms(x3, gfin_ref[...])
        o_ref[...] = x3


def _ffn(x, pe, g_ffn, w_up, conv_w, conv_b, w_down, g_ple, w_pg, w_pp, g_final, state, *, tb, seg, tf, final):
    n, d = x.shape
    dff = w_down.shape[0]
    nf = dff // tf
    with_state = state is not None
    nblk = n // tb
    nblk_seq = max(1, seg // tb)
    kern = functools.partial(_ffn_kernel, tb=tb, seg=seg, nblk_seq=nblk_seq, final=final, with_state=with_state)
    tok = lambda w: pl.BlockSpec((tb, w), lambda i, f: (i, 0))
    cst = lambda shape: pl.BlockSpec(shape, lambda i, f: (0,) * len(shape))
    in_specs = [tok(d), tok(pe.shape[1]), cst((1, d)),
                pl.BlockSpec((d, tf), lambda i, f: (0, f)), pl.BlockSpec((d, tf), lambda i, f: (0, nf + f)),
                pl.BlockSpec((CONV_W, tf), lambda i, f: (0, f)), pl.BlockSpec((CONV_W, tf), lambda i, f: (0, nf + f)),
                pl.BlockSpec((1, tf), lambda i, f: (0, f)), pl.BlockSpec((1, tf), lambda i, f: (0, nf + f)),
                pl.BlockSpec((tf, d), lambda i, f: (f, 0)),
                cst((1, d)), cst(w_pg.shape), cst(w_pp.shape), cst((1, d))]
    args = [x, pe, g_ffn, w_up, w_up, conv_w, conv_w, conv_b, conv_b, w_down, g_ple, w_pg, w_pp, g_final]
    if with_state:
        p1, p2 = state
        in_specs += [pl.BlockSpec((tb, tf), lambda i, f: (i, f)), pl.BlockSpec((tb, tf), lambda i, f: (i, nf + f)),
                     pl.BlockSpec((tb, tf), lambda i, f: (i, f)), pl.BlockSpec((tb, tf), lambda i, f: (i, nf + f))]
        args += [p1, p1, p2, p2]
        st_shape = jax.ShapeDtypeStruct((n, dff), F32)
        st_spec = pl.BlockSpec((tb, tf), lambda i, f: (i, f))
    else:
        st_shape = jax.ShapeDtypeStruct((nblk, CONV_W - 1, dff), F32)
        st_spec = pl.BlockSpec((None, CONV_W - 1, tf), lambda i, f: (i, 0, f))
    return pl.pallas_call(
        kern,
        grid=(nblk, nf),
        in_specs=in_specs,
        out_specs=[tok(d), st_spec, st_spec],
        out_shape=[jax.ShapeDtypeStruct((n, d), F32), st_shape, st_shape],
        scratch_shapes=[pltpu.VMEM((tb, d), BF), pltpu.VMEM((tb, d), F32),
                        pltpu.VMEM((nf, CONV_HALO, tf), F32), pltpu.VMEM((nf, CONV_HALO, tf), F32),
                        pltpu.VMEM((tb + CONV_HALO, tf), F32)],
        compiler_params=_cparams(("arbitrary", "arbitrary")),
        name="ffn",
    )(*args)


def _pick_block(n, target):
    b = min(n, target)
    while n % b:
        b //= 2
    return b


def kernel(x_prompt, x_sample, cache_fox_k, cache_fox_v, cache_fox_logf, cache_sb_k, cache_sb_v, state_pool, state_ffn_conv, page_table, p_prompt, p_sample, g_mix, w_in, b_f, gm_ln_g, gm_ln_b, gm_ws, gm_bs, pm_w, pm_scale, w_br, w_out, g_ffn, w_up, conv_w, conv_b, w_down, g_ple, w_ple_gate, w_ple_proj, g_final):
    nbp, t, d = x_prompt.shape
    nbs, ts, _ = x_sample.shape
    depth = w_in.shape[0]
    n_pool, page = cache_fox_k.shape[1], cache_fox_k.shape[2]
    n_pages = page_table.shape[1]
    past = n_pages * page
    dff = w_down.shape[1]
    f2 = 2 * dff
    off_mix = N_BRANCH * d
    off_f = off_mix + 6 * BR_WIDTH
    off_sbq = off_f + N_HEADS
    np_tok, ns_tok = nbp * t, nbs * ts

    tb_p = _pick_block(t, 512)
    tq = _pick_block(t, 256)
    tf = 256
    npp = _pick_block(n_pages, 8)

    row = lambda a: a.reshape(1, -1)
    xp = x_prompt.reshape(np_tok, d)
    xs = x_sample.reshape(ns_tok, d)
    kv_pages = lambda c: c.reshape(depth, n_pool, page, BR_WIDTH)
    ck_f, cv_f, ck_s, cv_s = kv_pages(cache_fox_k), kv_pages(cache_fox_v), kv_pages(cache_sb_k), kv_pages(cache_sb_v)
    clf = jnp.pad(jnp.swapaxes(cache_fox_logf, 2, 3), ((0, 0), (0, 0), (0, SUBLANES - N_HEADS), (0, 0)))

    gm_tile = lambda a: jnp.tile(a[:, :ts, :ts], (1, nbs, nbs))

    new_p = [[] for _ in range(7)]
    new_s = [[] for _ in range(8)]
    for i in range(depth):
        wi = w_in[i]
        w_mix = jnp.concatenate(
            [wi[:, off_mix:off_f], wi[:, off_sbq:], jnp.pad(wi[:, off_f:off_sbq], ((0, 0), (0, LANES - N_HEADS)))],
            axis=1).astype(BF)
        bf_row = jnp.pad(b_f[i], (0, LANES - N_HEADS)).reshape(1, LANES)
        w_gate = wi[:, :off_mix].astype(BF)
        w_br_b = w_br[i].astype(BF)
        w_out_b = w_out[i].astype(BF)
        w_up_b = w_up[i].astype(BF)
        w_down_b = w_down[i].astype(BF)
        w_pg = w_ple_gate[i].astype(BF)
        w_pp = w_ple_proj[i].astype(BF)
        w_pm_bd = jax.scipy.linalg.block_diag(*[pm_w[i, g] for g in range(len(POOL_WINDOWS))]).astype(BF)
        bs_full = lambda rows: jnp.repeat(jnp.tile(gm_bs[i][:, :min(rows, GM_CHUNK)].T, (rows // min(rows, GM_CHUNK), 1)),
                                          HEAD_DIM, axis=1)
        common = (row(g_mix[i]), w_mix, bf_row, row(gm_ln_g[i]), row(gm_ln_b[i]))
        final = i == depth - 1

        gm_rows = min(GM_CHUNK, tb_p)
        (o_gm, pool_in, fq, fk, fv, fkb, fvb, lf, _, cum_c, cum_r, sq, sk, sv, skb, svb, _) = _inproj(
            xp, *common, gm_ws[i][:, :gm_rows, :gm_rows], bs_full(gm_rows),
            tb=tb_p, seg=t, gm_rows=gm_rows, gm_chunk=GM_CHUNK)
        o_pool = _pool(pool_in, w_pm_bd, row(pm_scale[i]), tb=tb_p, rows_per_seq=t, offset=0, pos0=0)
        o_fox = _fox(fq, fkb, fvb, cum_c, cum_r, nseq=nbp, t=t, tq=tq)
        o_sb = _sb(sq, skb, svb, nseq=nbp, t=t, tq=tq)
        x1 = _merge(xp, o_gm, o_pool, o_fox, o_sb, row(g_mix[i]), w_gate, w_br_b, w_out_b, tb=tb_p)
        xp, st_g, st_v = _ffn(x1, p_prompt[i].reshape(np_tok, -1), row(g_ffn[i]), w_up_b, conv_w[i], row(conv_b[i]),
                              w_down_b, row(g_ple[i]), w_pg, w_pp, row(g_final), None,
                              tb=tb_p, seg=t, tf=tf, final=final)
        heads = lambda a, n, tt: a.reshape(n, tt, N_HEADS, HEAD_DIM)
        new_p[0].append(heads(fk, nbp, t))
        new_p[1].append(heads(fv, nbp, t))
        new_p[2].append(lf[:, :N_HEADS].reshape(nbp, t, N_HEADS))
        new_p[3].append(heads(sk, nbp, t))
        new_p[4].append(heads(sv, nbp, t))
        new_p[5].append(pool_in.reshape(nbp, t, BR_WIDTH)[:, t - POOL_BUF:])
        last = slice(t // tb_p - 1, None, t // tb_p)
        new_p[6].append(jnp.concatenate([st_g[last], st_v[last]], axis=-1))

        (o_gm, pool_in, fq, fk, fv, _, _, lf, lft, _, _, sq, sk, sv, _, _, vn) = _inproj(
            xs, *common, gm_tile(gm_ws[i]), bs_full_sample(gm_bs[i], ts, nbs),
            tb=ns_tok, seg=ts, gm_rows=ns_tok, gm_chunk=ts)
        grp_rows = -(-(POOL_BUF + ts) // SUBLANES) * SUBLANES
        lead = grp_rows - POOL_BUF - ts
        full = jnp.concatenate([jnp.zeros((nbs, lead, BR_WIDTH), F32), state_pool[i],
                                pool_in.reshape(nbs, ts, BR_WIDTH)], axis=1)
        o_pool = _pool(full.reshape(nbs * grp_rows, BR_WIDTH), w_pm_bd, row(pm_scale[i]),
                       tb=nbs * grp_rows, rows_per_seq=grp_rows, offset=grp_rows - ts, pos0=past)
        o_pool = o_pool.reshape(nbs, grp_rows, BR_WIDTH)[:, grp_rows - ts:].reshape(ns_tok, BR_WIDTH)
        o_fox, o_sb = _dec_attn(
            page_table, _block_diag_queries(fq, nbs, ts), _block_diag_queries(sq, nbs, ts),
            _pad_new(fk, nbs, ts), _pad_new(fv, nbs, ts),
            jnp.pad(jnp.swapaxes(lft.reshape(SUBLANES, nbs, ts), 0, 1), ((0, 0), (0, 0), (0, LANES - ts))),
            _pad_new(sk, nbs, ts), _pad_new(sv, nbs, ts),
            ck_f, cv_f, clf, ck_s, cv_s, layer=i, npp=npp, tnew=ts)
        o_fox = o_fox.reshape(ns_tok, BR_WIDTH).astype(BF)
        o_sb = o_sb.reshape(ns_tok, BR_WIDTH).astype(BF)
        x1 = _merge(xs, o_gm, o_pool, o_fox, o_sb, row(g_mix[i]), w_gate, w_br_b, w_out_b, tb=ns_tok)
        buf = state_ffn_conv[i]
        zero = jnp.zeros((nbs, ts - 1, f2), F32)
        p1 = jnp.concatenate([buf[:, 1:2], zero], axis=1).reshape(ns_tok, f2)
        p2 = jnp.concatenate([buf, zero[:, 1:]], axis=1).reshape(ns_tok, f2)
        xs, a_g, a_v = _ffn(x1, p_sample[i].reshape(ns_tok, -1), row(g_ffn[i]), w_up_b, conv_w[i], row(conv_b[i]),
                            w_down_b, row(g_ple[i]), w_pg, w_pp, row(g_final), (p1, p2),
                            tb=ns_tok, seg=ts, tf=tf, final=final)
        new_s[0].append(heads(fk, nbs, ts))
        new_s[1].append(heads(fv, nbs, ts))
        new_s[2].append(lf[:, :N_HEADS].reshape(nbs, ts, N_HEADS))
        new_s[3].append(heads(sk, nbs, ts))
        new_s[4].append(heads(sv, nbs, ts))
        new_s[5].append(jnp.concatenate([state_pool[i], pool_in.reshape(nbs, ts, BR_WIDTH)], axis=1)[:, ts:])
        a_full = jnp.concatenate([a_g, a_v], axis=-1).reshape(nbs, ts, f2)
        new_s[6].append(jnp.concatenate([buf, a_full], axis=1)[:, ts:])
        new_s[7].append(vn.reshape(nbs, ts, BR_WIDTH))

    stk = lambda lst: jnp.stack(lst, axis=0)
    return (xp.reshape(nbp, t, d), xs.reshape(nbs, ts, d),
            *[stk(a) for a in new_p], *[stk(a) for a in new_s])


def bs_full_sample(gm_bs_i, ts, nbs):
    return jnp.repeat(jnp.tile(gm_bs_i[:, :ts].T, (nbs, 1)), HEAD_DIM, axis=1)


def _block_diag_queries(q, nbs, ts):
    q = q.reshape(nbs, ts, 1, N_HEADS, HEAD_DIM)
    eye = jnp.eye(SUBLANES, N_HEADS, dtype=q.dtype).reshape(1, 1, SUBLANES, N_HEADS, 1)
    return (q * eye).reshape(nbs, ts * SUBLANES, BR_WIDTH)


def _pad_new(a, nbs, ts):
    return jnp.pad(a.reshape(nbs, ts, BR_WIDTH), ((0, 0), (0, LANES - ts), (0, 0)))
```

```python
import functools

import jax
import jax.numpy as jnp
from jax import lax
from jax.experimental import pallas as pl
from jax.experimental.pallas import tpu as pltpu

F32 = jnp.float32
BF = jnp.bfloat16

EPS = 1e-6
N_BRANCH = 4
BR_WIDTH = 256
N_HEADS = 4
HEAD_DIM = 64
LANES = 128
SUBLANES = 8
POOL_WINDOWS = (2, 4, 8, 16)
POOL_BUF = 15
POOL_HALO = 16
CONV_W = 3
GM_CHUNK = 128
NEG_BIG = -1e30
EXP_ZERO_BELOW = -104.0
VMEM_LIMIT = 48 * 1024 * 1024

C_U, C_V, C_POOL, C_FQ, C_SQ, C_F = [BR_WIDTH * i for i in range(6)]
N_TOKMAJ = C_F + LANES
R_FK, R_FV, R_SK, R_SV = [BR_WIDTH * i for i in range(4)]
N_HEADMAJ = 4 * BR_WIDTH
W_BLOCK = 128


def _cparams(sem):
    return pltpu.CompilerParams(dimension_semantics=sem, vmem_limit_bytes=VMEM_LIMIT)


def _rms(x, g):
    return x * lax.rsqrt(jnp.mean(x * x, axis=-1, keepdims=True) + EPS) * g


def _softplus(x):
    return jnp.maximum(x, 0.0) + jnp.log1p(jnp.exp(-jnp.abs(x)))


def _split3(x):
    hi = x.astype(BF)
    r = x - hi.astype(F32)
    mid = r.astype(BF)
    r = r - mid.astype(F32)
    return hi, mid, r.astype(BF)


def _dot(a, b):
    return jnp.dot(a, b, preferred_element_type=F32)


def _dot_nt(a, b):
    return lax.dot_general(a, b, (((1,), (1,)), ((), ())), preferred_element_type=F32)


def _dot_exact(x, ones_bf, left):
    parts = _split3(x)
    if left:
        return _dot(ones_bf, parts[0]) + _dot(ones_bf, parts[1]) + _dot(ones_bf, parts[2])
    return _dot(parts[0], ones_bf) + _dot(parts[1], ones_bf) + _dot(parts[2], ones_bf)


def _const_spec(shape):
    nd = len(shape)
    return pl.BlockSpec(shape, lambda *_: (0,) * nd)


def _wprep_kernel(st_ref, w_ref, o_ref):
    del st_ref
    for layer in range(o_ref.shape[0]):
        o_ref[layer] = w_ref[:, layer, :].astype(BF)


def _wprep(w_t, starts):
    n_out, depth, d = w_t.shape
    nblk = len(starts)
    grid_spec = pltpu.PrefetchScalarGridSpec(
        num_scalar_prefetch=1,
        grid=(nblk,),
        in_specs=[pl.BlockSpec((pl.Element(W_BLOCK), pl.Element(depth), pl.Element(d)), lambda j, st: (st[j], 0, 0))],
        out_specs=pl.BlockSpec((depth, W_BLOCK, d), lambda j, st: (0, j, 0)),
    )
    return pl.pallas_call(
        _wprep_kernel,
        grid_spec=grid_spec,
        out_shape=jax.ShapeDtypeStruct((depth, nblk * W_BLOCK, d), BF),
        compiler_params=_cparams(("arbitrary",)),
        name="wprep",
    )(jnp.asarray(starts, jnp.int32), w_t)


def _inproj_kernel(x_ref, g_ref, w_ref, bfr_ref, lng_ref, lnb_ref, ws_ref, bs_ref,
                   ogm_ref, pool_ref, fq_ref, sq_ref, vn_ref, cc_ref, lft_ref, cr_ref,
                   fk_ref, fv_ref, sk_ref, sv_ref, fkb_ref, fvb_ref, skb_ref, svb_ref,
                   carc_ref, *, tb, seg, gm_rows, gm_chunk):
    i = pl.program_id(0)
    h = _rms(x_ref[...], g_ref[...]).astype(BF)
    z = _dot_nt(h, w_ref[0:N_TOKMAJ, :])
    zt = _dot_nt(w_ref[N_TOKMAJ:N_TOKMAJ + N_HEADMAJ, :], h)

    r = lax.broadcasted_iota(jnp.int32, (gm_rows, gm_rows), 0)
    c = lax.broadcasted_iota(jnp.int32, (gm_rows, gm_rows), 1)
    mix_mask = (r >= c) & ((r // gm_chunk) == (c // gm_chunk))
    grp = lax.broadcasted_iota(jnp.int32, (gm_rows, BR_WIDTH), 1) // HEAD_DIM
    for blk in range(tb // gm_rows):
        rows = slice(blk * gm_rows, (blk + 1) * gm_rows)
        u = z[rows, C_U:C_U + BR_WIDTH]
        v = z[rows, C_V:C_V + BR_WIDTH]
        mu = jnp.mean(v, axis=-1, keepdims=True)
        var = jnp.mean(jnp.square(v - mu), axis=-1, keepdims=True)
        vn = (v - mu) * lax.rsqrt(var + EPS) * lng_ref[...] + lnb_ref[...]
        vn_ref[rows, :] = vn
        vnb = vn.astype(BF)
        s = bs_ref[...]
        for g in range(N_HEADS):
            wg = jnp.where(mix_mask, ws_ref[g], 0.0).astype(BF)
            s = s + jnp.where(grp == g, _dot(wg, vnb), 0.0)
        ogm_ref[rows, :] = (u * s).astype(BF)

    pool_ref[...] = z[:, C_POOL:C_POOL + BR_WIDTH]
    scale = HEAD_DIM ** -0.5
    fq_ref[...] = (z[:, C_FQ:C_FQ + BR_WIDTH] * scale).astype(BF)
    sq_ref[...] = (z[:, C_SQ:C_SQ + BR_WIDTH] * scale).astype(BF)
    for start, full_ref, half_ref in ((R_FK, fk_ref, fkb_ref), (R_FV, fv_ref, fvb_ref),
                                      (R_SK, sk_ref, skb_ref), (R_SV, sv_ref, svb_ref)):
        part = zt[start:start + BR_WIDTH, :]
        full_ref[...] = part
        half_ref[...] = part.astype(BF)

    lf = -_softplus(-(z[:, C_F:C_F + LANES] + bfr_ref[...]))
    lft_ref[...] = lf.T[0:SUBLANES, :]
    r = lax.broadcasted_iota(jnp.int32, (tb, tb), 0)
    c = lax.broadcasted_iota(jnp.int32, (tb, tb), 1)
    if seg >= tb:
        lower = (r >= c)
        nblk_seq = seg // tb

        @pl.when(i % nblk_seq == 0)
        def _():
            carc_ref[...] = jnp.zeros_like(carc_ref)
    else:
        lower = (r >= c) & ((r // seg) == (c // seg))
        carc_ref[...] = jnp.zeros_like(carc_ref)
    cum_c = _dot_exact(lf, jnp.where(lower, 1.0, 0.0).astype(BF), left=True) + carc_ref[0:1, :]
    cc_ref[...] = cum_c
    cr_ref[...] = cum_c.T[0:SUBLANES, :]
    carc_ref[...] = jnp.broadcast_to(cum_c[tb - 1:tb, :], carc_ref.shape)


def _inproj(x, g_mix, w_mix, bf_row, ln_g, ln_b, ws, bs, *, layer, tb, seg, gm_rows, gm_chunk, nseq):
    n, d = x.shape
    kern = functools.partial(_inproj_kernel, tb=tb, seg=seg, gm_rows=gm_rows, gm_chunk=gm_chunk)
    tok = lambda w: pl.BlockSpec((tb, w), lambda i: (i, 0))
    per_seq = n // nseq
    nblk_seq = per_seq // tb
    chan = lambda rows: pl.BlockSpec((None, rows, tb), lambda i: (i // nblk_seq, 0, i % nblk_seq))
    tokmaj = lambda width, dt: jax.ShapeDtypeStruct((n, width), dt)
    chanmaj = lambda rows, dt: jax.ShapeDtypeStruct((nseq, rows, per_seq), dt)
    out_shape = [
        tokmaj(BR_WIDTH, BF),
        tokmaj(BR_WIDTH, F32),
        tokmaj(BR_WIDTH, BF),
        tokmaj(BR_WIDTH, BF),
        tokmaj(BR_WIDTH, F32),
        tokmaj(LANES, F32),
        chanmaj(SUBLANES, F32),
        chanmaj(SUBLANES, F32),
    ] + [chanmaj(BR_WIDTH, F32)] * 4 + [chanmaj(BR_WIDTH, BF)] * 4
    out_specs = ([tok(BR_WIDTH)] * 5 + [tok(LANES), chan(SUBLANES), chan(SUBLANES)] + [chan(BR_WIDTH)] * 8)
    w_rows = w_mix.shape[1]
    return pl.pallas_call(
        kern,
        grid=(n // tb,),
        in_specs=[tok(d), _const_spec((1, d)),
                  pl.BlockSpec((None, w_rows, d), lambda i: (layer, 0, 0), pipeline_mode=pl.Buffered(1)),
                  _const_spec(bf_row.shape), _const_spec(ln_g.shape),
                  _const_spec(ln_b.shape), _const_spec(ws.shape), _const_spec(bs.shape)],
        out_specs=out_specs,
        out_shape=out_shape,
        scratch_shapes=[pltpu.VMEM((SUBLANES, LANES), F32)],
        compiler_params=_cparams(("arbitrary",)),
        name="inproj",
    )(x, g_mix, w_mix, bf_row, ln_g, ln_b, ws, bs)


def _pool_kernel(x_ref, w_ref, sc_ref, o_ref, ext_ref, *, tb, rows_per_seq, offset, pos0, nblk_seq):
    i = pl.program_id(0)

    @pl.when(i % nblk_seq == 0)
    def _():
        ext_ref[0:POOL_HALO, :] = jnp.zeros((POOL_HALO, BR_WIDTH), F32)

    ext_ref[POOL_HALO:POOL_HALO + tb, :] = x_ref[...]
    rowi = lax.broadcasted_iota(jnp.int32, (tb, LANES), 0)
    lane = lax.broadcasted_iota(jnp.int32, (tb, LANES), 1)
    pos = pos0 + (i * tb + rowi) % rows_per_seq - offset
    first = lane < HEAD_DIM
    halves = []
    for half in range(2):
        cols = slice(half * LANES, (half + 1) * LANES)
        shifted = lambda j: ext_ref[POOL_HALO - j:POOL_HALO - j + tb, cols]
        w_small, w_big = POOL_WINDOWS[2 * half], POOL_WINDOWS[2 * half + 1]
        cur = shifted(0)
        acc = cur
        for j in range(1, w_small):
            acc = acc + shifted(j)
        small = acc
        for j in range(w_small, w_big):
            acc = acc + shifted(j)
        win = jnp.where(first, small, acc)
        width = jnp.where(first, w_small, w_big)
        cnt = jnp.clip(pos + 1, 1, width).astype(F32)
        halves.append(win / cnt - cur)
    d = jnp.concatenate(halves, axis=1).astype(BF)
    o_ref[...] = (_dot(d, w_ref[...]) * sc_ref[...]).astype(BF)
    ext_ref[0:POOL_HALO, :] = ext_ref[tb:tb + POOL_HALO, :]


def _pool(x, w_bd, scale, *, tb, rows_per_seq, offset, pos0):
    n = x.shape[0]
    kern = functools.partial(_pool_kernel, tb=tb, rows_per_seq=rows_per_seq, offset=offset, pos0=pos0,
                             nblk_seq=max(1, rows_per_seq // tb))
    return pl.pallas_call(
        kern,
        grid=(n // tb,),
        in_specs=[pl.BlockSpec((tb, BR_WIDTH), lambda i: (i, 0)), _const_spec(w_bd.shape), _const_spec(scale.shape)],
        out_specs=pl.BlockSpec((tb, BR_WIDTH), lambda i: (i, 0)),
        out_shape=jax.ShapeDtypeStruct((n, BR_WIDTH), BF),
        scratch_shapes=[pltpu.VMEM((tb + POOL_HALO, BR_WIDTH), F32)],
        compiler_params=_cparams(("arbitrary",)),
        name="pool",
    )(x, w_bd, scale)


def _head_queries(q):
    left = lax.broadcasted_iota(jnp.int32, (q.shape[0], LANES), 1) < HEAD_DIM
    out = []
    for h in range(N_HEADS):
        pair = q[:, (h // 2) * LANES:(h // 2 + 1) * LANES]
        keep = left if h % 2 == 0 else jnp.logical_not(left)
        out.append(jnp.where(keep, pair, jnp.zeros_like(pair)))
    return out, left


def _fox_kernel(q_ref, k_ref, v_ref, cc_ref, cr_ref, o_ref, kn_ref, *, tq):
    qb = pl.program_id(1)
    qh, left = _head_queries(q_ref[...])
    cc = cc_ref[...]
    cq = [cc[:, h:h + 1] for h in range(N_HEADS)]
    r = lax.broadcasted_iota(jnp.int32, (tq, tq), 0)
    c = lax.broadcasted_iota(jnp.int32, (tq, tq), 1)
    causal = c <= r

    @pl.when(qb == 0)
    def _():
        hr = lax.broadcasted_iota(jnp.int32, (SUBLANES, BR_WIDTH), 0)
        hc = lax.broadcasted_iota(jnp.int32, (SUBLANES, BR_WIDTH), 1) // HEAD_DIM
        head_rows = jnp.where(hr == hc, 1.0, 0.0).astype(BF)

        def chunk(ci, best):
            kc = k_ref[:, pl.ds(pl.multiple_of(ci * tq, tq), tq)].astype(F32)
            n2 = _dot_exact(kc * kc, head_rows, left=True)
            return jnp.maximum(best, jnp.max(n2, axis=1, keepdims=True))

        best = lax.fori_loop(0, k_ref.shape[1] // tq, chunk, jnp.zeros((SUBLANES, 1), F32))
        kn_ref[...] = jnp.broadcast_to(best, kn_ref.shape)

    reach = []
    for h in range(N_HEADS):
        qf = qh[h].astype(F32)
        qn2 = jnp.sum(qf * qf, axis=-1, keepdims=True)
        reach.append(jnp.sqrt(qn2 * kn_ref[h:h + 1, 0:1]) * 1.001 + 0.01 + cq[h])

    def block(j, carry, masked):
        ms, ls, accs = carry
        start = pl.multiple_of(j * tq, tq)
        kblk = k_ref[:, pl.ds(start, tq)]
        vblk = v_ref[:, pl.ds(start, tq)]
        new_m, new_l, pv, alphas = [], [], [], []
        for h in range(N_HEADS):
            pair = slice((h // 2) * LANES, (h // 2 + 1) * LANES)
            ck = cr_ref[h:h + 1, pl.ds(start, tq)]
            s = _dot(qh[h], kblk[pair, :]) + (cq[h] - ck)
            if masked:
                s = jnp.where(causal, s, NEG_BIG)
            m_new = jnp.maximum(ms[h], jnp.max(s, axis=-1, keepdims=True))
            alpha = jnp.exp(ms[h] - m_new)
            p = jnp.exp(s - m_new)
            new_m.append(m_new)
            new_l.append(ls[h] * alpha + jnp.sum(p, axis=-1, keepdims=True))
            alphas.append(alpha)
            pv.append(_dot_nt(p.astype(BF), vblk[pair, :]))
        new_acc = []
        for hp in range(N_HEADS // 2):
            a = jnp.where(left, alphas[2 * hp], alphas[2 * hp + 1])
            new_acc.append(accs[hp] * a + jnp.where(left, pv[2 * hp], pv[2 * hp + 1]))
        return tuple(new_m), tuple(new_l), tuple(new_acc)

    init = (tuple(jnp.full((tq, 1), NEG_BIG, F32) for _ in range(N_HEADS)),
            tuple(jnp.zeros((tq, 1), F32) for _ in range(N_HEADS)),
            tuple(jnp.zeros((tq, LANES), F32) for _ in range(N_HEADS // 2)))
    def live(state):
        j, ms = state[0], state[1]
        start = pl.multiple_of(jnp.maximum(j, 0) * tq, tq)
        gap = None
        for h in range(N_HEADS):
            ck_last = cr_ref[h:h + 1, pl.ds(start, tq)][:, tq - 1:tq]
            g = reach[h] - ck_last - ms[h]
            gap = g if gap is None else jnp.maximum(gap, g)
        return (j >= 0) & (jnp.max(gap) > EXP_ZERO_BELOW)

    def step(state):
        j = state[0]
        return (j - 1,) + block(j, state[1:], False)

    first = block(qb, init, True)
    _, ms, ls, accs = lax.while_loop(live, step, (qb - 1,) + first)
    for hp in range(N_HEADS // 2):
        den = jnp.where(left, ls[2 * hp], ls[2 * hp + 1])
        o_ref[:, hp * LANES:(hp + 1) * LANES] = (accs[hp] / den).astype(BF)


def _fox(q, kb, vb, cum_c, cum_r, *, nseq, t, tq):
    kern = functools.partial(_fox_kernel, tq=tq)
    nq = t // tq
    return pl.pallas_call(
        kern,
        grid=(nseq, nq),
        in_specs=[pl.BlockSpec((tq, BR_WIDTH), lambda n, i: (n * nq + i, 0)),
                  pl.BlockSpec((None, BR_WIDTH, t), lambda n, i: (n, 0, 0)),
                  pl.BlockSpec((None, BR_WIDTH, t), lambda n, i: (n, 0, 0)),
                  pl.BlockSpec((tq, LANES), lambda n, i: (n * nq + i, 0)),
                  pl.BlockSpec((None, SUBLANES, t), lambda n, i: (n, 0, 0))],
        out_specs=pl.BlockSpec((tq, BR_WIDTH), lambda n, i: (n * nq + i, 0)),
        out_shape=jax.ShapeDtypeStruct((nseq * t, BR_WIDTH), BF),
        scratch_shapes=[pltpu.VMEM((SUBLANES, LANES), F32)],
        compiler_params=_cparams(("arbitrary", "arbitrary")),
        name="fox_prompt",
    )(q, kb, vb, cum_c, cum_r)


def _sb_kernel(q_ref, k_ref, v_ref, o_ref, *, tq):
    qb = pl.program_id(1)
    qh, left = _head_queries(q_ref[...])
    r = lax.broadcasted_iota(jnp.int32, (tq, tq), 0)
    c = lax.broadcasted_iota(jnp.int32, (tq, tq), 1)
    strict = c < r
    after = jnp.where(r > c, 1.0, 0.0).astype(BF)

    def block(j, carry, masked):
        cs, accs = carry
        start = pl.multiple_of(j * tq, tq)
        kblk = k_ref[:, pl.ds(start, tq)]
        vblk = v_ref[:, pl.ds(start, tq)]
        new_c, av = [], []
        for h in range(N_HEADS):
            pair = slice((h // 2) * LANES, (h // 2 + 1) * LANES)
            z = _dot(qh[h], kblk[pair, :])
            lrest = -_softplus(z)
            if masked:
                lrest = jnp.where(strict, lrest, 0.0)
            later = _dot_exact(lrest, after, left=False) + cs[h]
            a = jnp.exp(z + lrest + later)
            if masked:
                a = jnp.where(strict, a, 0.0)
            av.append(_dot_nt(a.astype(BF), vblk[pair, :]))
            new_c.append(cs[h] + jnp.sum(lrest, axis=-1, keepdims=True))
        new_acc = tuple(accs[hp] + jnp.where(left, av[2 * hp], av[2 * hp + 1]) for hp in range(N_HEADS // 2))
        return tuple(new_c), new_acc

    init = (tuple(jnp.zeros((tq, 1), F32) for _ in range(N_HEADS)),
            tuple(jnp.zeros((tq, LANES), F32) for _ in range(N_HEADS // 2)))
    def live(state):
        j, cs = state[0], state[1]
        top = cs[0]
        for h in range(1, N_HEADS):
            top = jnp.maximum(top, cs[h])
        return (j >= 0) & (jnp.max(top) > EXP_ZERO_BELOW)

    def step(state):
        j = state[0]
        return (j - 1,) + block(j, state[1:], False)

    first = block(qb, init, True)
    _, _, accs = lax.while_loop(live, step, (qb - 1,) + first)
    for hp in range(N_HEADS // 2):
        o_ref[:, hp * LANES:(hp + 1) * LANES] = accs[hp].astype(BF)


def _sb(q, kb, vb, *, nseq, t, tq):
    kern = functools.partial(_sb_kernel, tq=tq)
    nq = t // tq
    return pl.pallas_call(
        kern,
        grid=(nseq, nq),
        in_specs=[pl.BlockSpec((tq, BR_WIDTH), lambda n, i: (n * nq + i, 0)),
                  pl.BlockSpec((None, BR_WIDTH, t), lambda n, i: (n, 0, 0)),
                  pl.BlockSpec((None, BR_WIDTH, t), lambda n, i: (n, 0, 0))],
        out_specs=pl.BlockSpec((tq, BR_WIDTH), lambda n, i: (n * nq + i, 0)),
        out_shape=jax.ShapeDtypeStruct((nseq * t, BR_WIDTH), BF),
        compiler_params=_cparams(("arbitrary", "arbitrary")),
        name="sb_prompt",
    )(q, kb, vb)


def _dec_attn_kernel(pt_ref, qf_ref, qs_ref, knf_ref, vnf_ref, lfn_ref, kns_ref, vns_ref, *rest, npp, tnew):
    del pt_ref
    DEC_ROWS = tnew * SUBLANES
    pages = rest[:5 * npp]
    of_ref, os_ref = rest[5 * npp:5 * npp + 2]
    m_ref, l_ref, accf_ref, offf_ref, e_ref, cs_ref, accs_ref = rest[5 * npp + 2:]
    s_id = pl.program_id(1)
    n_steps = pl.num_programs(1)

    r = lax.broadcasted_iota(jnp.int32, (LANES, LANES), 0)
    c = lax.broadcasted_iota(jnp.int32, (LANES, LANES), 1)
    after = jnp.where(r > c, 1.0, 0.0).astype(BF)
    rowi = lax.broadcasted_iota(jnp.int32, (DEC_ROWS, LANES), 0)
    lane = lax.broadcasted_iota(jnp.int32, (DEC_ROWS, LANES), 1)
    t_row = rowi // SUBLANES

    def suffix(l_list):
        nb = len(l_list)
        stack = l_list[0] if nb == 1 else jnp.concatenate(l_list, axis=0)
        inner = _dot_exact(stack, after, left=False)
        out = [None] * nb
        off = jnp.zeros((DEC_ROWS, 1), F32)
        for b in range(nb - 1, -1, -1):
            out[b] = inner[b * DEC_ROWS:(b + 1) * DEC_ROWS] + off
            off = off + jnp.sum(l_list[b], axis=-1, keepdims=True)
        return out, off

    def fox_part(kf, vf, lf8, is_new):
        nb = len(kf)
        qf = qf_ref[...]
        lf = [jnp.concatenate([x] * (DEC_ROWS // SUBLANES), axis=0) for x in lf8]
        rsum, tot = suffix(lf)
        base = offf_ref[...] - e_ref[...]
        s_list = []
        for b in range(nb):
            s = _dot(qf, kf[b].astype(BF)) + rsum[b] + base
            if is_new:
                s = jnp.where((lane <= t_row) & (lane < tnew), s, NEG_BIG)
            s_list.append(s)
        m_old = m_ref[...]
        m_blk = s_list[0]
        for b in range(1, nb):
            m_blk = jnp.maximum(m_blk, s_list[b])
        m_new = jnp.maximum(m_old, jnp.max(m_blk, axis=-1, keepdims=True))
        alpha = jnp.exp(m_old - m_new)
        psum = jnp.zeros((DEC_ROWS, LANES), F32)
        pv = jnp.zeros((DEC_ROWS, BR_WIDTH), F32)
        for b in range(nb):
            p = jnp.exp(s_list[b] - m_new)
            psum = psum + p
            pv = pv + _dot_nt(p.astype(BF), vf[b].astype(BF))
        m_ref[...] = m_new
        l_ref[...] = l_ref[...] * alpha + jnp.sum(psum, axis=-1, keepdims=True)
        accf_ref[...] = accf_ref[...] * alpha[:, 0:1] + pv
        offf_ref[...] = offf_ref[...] + tot

    def sb_part(ks, vs, is_new):
        nb = len(ks)
        qs = qs_ref[...]
        z_list, l_list = [], []
        for b in range(nb):
            z = _dot(qs, ks[b].astype(BF))
            lrest = -_softplus(z)
            if is_new:
                lrest = jnp.where((lane < t_row) & (lane < tnew), lrest, 0.0)
            z_list.append(z)
            l_list.append(lrest)
        later, tot_s = suffix(l_list)
        cs = cs_ref[...]
        av = jnp.zeros((DEC_ROWS, BR_WIDTH), F32)
        for b in range(nb):
            a = jnp.exp(z_list[b] + l_list[b] + later[b] + cs)
            if is_new:
                a = jnp.where((lane < t_row) & (lane < tnew), a, 0.0)
            av = av + _dot_nt(a.astype(BF), vs[b].astype(BF))
        accs_ref[...] = accs_ref[...] + av
        cs_ref[...] = cs + tot_s

    @pl.when(s_id == 0)
    def _():
        m_ref[...] = jnp.full(m_ref.shape, NEG_BIG, F32)
        l_ref[...] = jnp.zeros_like(l_ref)
        accf_ref[...] = jnp.zeros_like(accf_ref)
        offf_ref[...] = jnp.zeros_like(offf_ref)
        cs_ref[...] = jnp.zeros_like(cs_ref)
        accs_ref[...] = jnp.zeros_like(accs_ref)
        lfn = jnp.concatenate([lfn_ref[...]] * (DEC_ROWS // SUBLANES), axis=0)
        inner = _dot_exact(lfn, after, left=False)
        e = jnp.sum(jnp.where(lane == t_row, inner, 0.0), axis=-1, keepdims=True)
        e_ref[...] = jnp.broadcast_to(e, e_ref.shape)
        fox_part([knf_ref[...]], [vnf_ref[...]], [lfn_ref[...]], True)
        sb_part([kns_ref[...]], [vns_ref[...]], True)

    get = lambda k: [pages[5 * j + k][...] for j in range(npp)]

    @pl.when(s_id > 0)
    def _():
        fox_part(get(0), get(1), get(2), False)

    @pl.when((s_id > 0) & (jnp.max(cs_ref[...]) > EXP_ZERO_BELOW))
    def _():
        sb_part(get(3), get(4), False)

    @pl.when(s_id == n_steps - 1)
    def _():
        own = (lax.broadcasted_iota(jnp.int32, (DEC_ROWS, BR_WIDTH), 1) // HEAD_DIM
               == lax.broadcasted_iota(jnp.int32, (DEC_ROWS, BR_WIDTH), 0) % SUBLANES)
        nt = DEC_ROWS // SUBLANES
        fo = jnp.where(own, accf_ref[...] / l_ref[:, 0:1], 0.0).reshape(nt, SUBLANES, BR_WIDTH)
        so = jnp.where(own, accs_ref[...], 0.0).reshape(nt, SUBLANES, BR_WIDTH)
        of_ref[...] = jnp.sum(fo, axis=1)
        os_ref[...] = jnp.sum(so, axis=1)


def _dec_attn(page_table, qf, qs, knf, vnf, lfn, kns, vns, ck_f, cv_f, clf, ck_s, cv_s, *, layer, npp, tnew):
    nb, n_pages = page_table.shape
    page = ck_f.shape[3]
    DEC_ROWS = tnew * SUBLANES
    n_steps = 1 + n_pages // npp
    kern = functools.partial(_dec_attn_kernel, npp=npp, tnew=tnew)
    per_b = lambda shape: pl.BlockSpec((None,) + shape, lambda b, s, pt: (b, 0, 0))

    def page_spec(j, rows, width):
        def idx(b, s, pt):
            return (layer, pt[b, n_pages - jnp.maximum(s, 1) * npp + j], 0, 0)
        return pl.BlockSpec((None, None, rows, width), idx)

    in_specs = [per_b((DEC_ROWS, BR_WIDTH)), per_b((DEC_ROWS, BR_WIDTH)),
                per_b((BR_WIDTH, LANES)), per_b((BR_WIDTH, LANES)), per_b((SUBLANES, LANES)),
                per_b((BR_WIDTH, LANES)), per_b((BR_WIDTH, LANES))]
    args = [qf, qs, knf, vnf, lfn, kns, vns]
    for j in range(npp):
        in_specs += [page_spec(j, BR_WIDTH, page), page_spec(j, BR_WIDTH, page), page_spec(j, SUBLANES, page),
                     page_spec(j, BR_WIDTH, page), page_spec(j, BR_WIDTH, page)]
        args += [ck_f, cv_f, clf, ck_s, cv_s]
    nt = DEC_ROWS // SUBLANES
    out_spec = pl.BlockSpec((None, nt, BR_WIDTH), lambda b, s, pt: (b, 0, 0))
    wide = pltpu.VMEM((DEC_ROWS, BR_WIDTH), F32)
    narrow = pltpu.VMEM((DEC_ROWS, LANES), F32)
    grid_spec = pltpu.PrefetchScalarGridSpec(
        num_scalar_prefetch=1,
        grid=(nb, n_steps),
        in_specs=in_specs,
        out_specs=[out_spec, out_spec],
        scratch_shapes=[narrow, narrow, wide, narrow, narrow, narrow, wide],
    )
    return pl.pallas_call(
        kern,
        grid_spec=grid_spec,
        out_shape=[jax.ShapeDtypeStruct((nb, nt, BR_WIDTH), F32)] * 2,
        compiler_params=_cparams(("arbitrary", "arbitrary")),
        name="dec_attn",
    )(page_table, *args)


def _merge_kernel(x_ref, og_ref, op_ref, of_ref, os_ref, g_ref, wg_ref, wb_ref, wo_ref, o_ref):
    x = x_ref[...]
    d = x.shape[1]
    h = _rms(x, g_ref[...]).astype(BF)
    acc = jnp.zeros(x.shape, F32)
    for b, ref in enumerate((og_ref, op_ref, of_ref, os_ref)):
        gate = jax.nn.sigmoid(_dot_nt(h, wg_ref[b * d:(b + 1) * d, :]))
        acc = acc + gate * _dot(ref[...], wb_ref[b])
    o_ref[...] = x + _dot(acc.astype(BF), wo_ref[...])


def _resident(shape):
    nd = len(shape)
    return pl.BlockSpec(shape, lambda *_: (0,) * nd, pipeline_mode=pl.Buffered(1))


def _merge(x, o_gm, o_pool, o_fox, o_sb, g_mix, w_gate, w_br, w_out, *, layer, tb):
    n, d = x.shape
    tok = lambda w: pl.BlockSpec((tb, w), lambda i: (i, 0))
    gate_spec = pl.BlockSpec((None,) + w_gate.shape[1:], lambda i: (layer, 0, 0), pipeline_mode=pl.Buffered(1))
    return pl.pallas_call(
        _merge_kernel,
        grid=(n // tb,),
        in_specs=[tok(d), tok(BR_WIDTH), tok(BR_WIDTH), tok(BR_WIDTH), tok(BR_WIDTH), _const_spec((1, d)),
                  gate_spec, _resident(w_br.shape), _resident(w_out.shape)],
        out_specs=tok(d),
        out_shape=jax.ShapeDtypeStruct((n, d), F32),
        compiler_params=_cparams(("arbitrary",)),
        name="merge",
    )(x, o_gm, o_pool, o_fox, o_sb, g_mix, w_gate, w_br, w_out)


CONV_HALO = 8


def _gelu_tanh(x):
    return 0.5 * x * (1.0 + jnp.tanh(0.7978845608028654 * (x + 0.044715 * (x * x * x))))


def _ffn_kernel(*refs, tb, seg, nblk_seq, final, with_state):
    (x_ref, pe_ref, gf_ref, wug_ref, wuv_ref, cwg_ref, cwv_ref, cbg_ref, cbv_ref, wd_ref,
     gp_ref, wpg_ref, wpp_ref, gfin_ref) = refs[:14]
    refs = refs[14:]
    if with_state:
        p1g_ref, p1v_ref, p2g_ref, p2v_ref = refs[:4]
        refs = refs[4:]
    o_ref, sg_ref, sv_ref, h_ref, acc_ref, cg_ref, cv_ref, ext_ref = refs
    i = pl.program_id(0)
    f = pl.program_id(1)
    nf = pl.num_programs(1)

    @pl.when(f == 0)
    def _():
        h_ref[...] = _rms(x_ref[...], gf_ref[...]).astype(BF)
        acc_ref[...] = jnp.zeros_like(acc_ref)

    @pl.when(i % nblk_seq == 0)
    def _():
        cg_ref[f] = jnp.zeros((CONV_HALO, cg_ref.shape[2]), F32)
        cv_ref[f] = jnp.zeros((CONV_HALO, cv_ref.shape[2]), F32)

    h = h_ref[...]
    tmod = lax.broadcasted_iota(jnp.int32, (tb, wug_ref.shape[1]), 0) % seg

    def conv(w_ref, cw_ref, cb_ref, carry_ref, state):
        a = _dot(h, w_ref[...])
        ext_ref[0:CONV_HALO, :] = carry_ref[f]
        ext_ref[CONV_HALO:CONV_HALO + tb, :] = a
        prev1 = ext_ref[CONV_HALO - 1:CONV_HALO - 1 + tb, :]
        prev2 = ext_ref[CONV_HALO - 2:CONV_HALO - 2 + tb, :]
        if with_state:
            prev1 = jnp.where(tmod < 1, state[0][...], prev1)
            prev2 = jnp.where(tmod < 2, state[1][...], prev2)
        carry_ref[f] = ext_ref[tb:tb + CONV_HALO, :]
        cw = cw_ref[...]
        return a, cb_ref[...] + prev2 * cw[0:1, :] + prev1 * cw[1:2, :] + a * cw[2:3, :]

    ag, cgate = conv(wug_ref, cwg_ref, cbg_ref, cg_ref, (p1g_ref, p2g_ref) if with_state else None)
    av, cval = conv(wuv_ref, cwv_ref, cbv_ref, cv_ref, (p1v_ref, p2v_ref) if with_state else None)
    if with_state:
        sg_ref[...] = ag
        sv_ref[...] = av
    else:
        sg_ref[...] = ag[tb - (CONV_W - 1):tb, :]
        sv_ref[...] = av[tb - (CONV_W - 1):tb, :]
    acc_ref[...] += _dot((_gelu_tanh(cgate) * cval).astype(BF), wd_ref[...])

    @pl.when(f == nf - 1)
    def _():
        x2 = x_ref[...] + acc_ref[...]
        gate = jax.nn.sigmoid(_dot(_rms(x2, gp_ref[...]).astype(BF), wpg_ref[...]))
        x3 = x2 + gate * _dot(pe_ref[...].astype(BF), wpp_ref[...])
        if final:
            x3 = _rms(x3, gfin_ref[...])
        o_ref[...] = x3


def _ffn(x, pe, g_ffn, w_up, conv_w, conv_b, w_down, g_ple, w_pg, w_pp, g_final, state, *, tb, seg, tf, final):
    n, d = x.shape
    dff = w_down.shape[0]
    nf = dff // tf
    with_state = state is not None
    nblk = n // tb
    nblk_seq = max(1, seg // tb)
    kern = functools.partial(_ffn_kernel, tb=tb, seg=seg, nblk_seq=nblk_seq, final=final, with_state=with_state)
    tok = lambda w: pl.BlockSpec((tb, w), lambda i, f: (i, 0))
    cst = lambda shape: pl.BlockSpec(shape, lambda i, f: (0,) * len(shape))
    in_specs = [tok(d), tok(pe.shape[1]), cst((1, d)),
                pl.BlockSpec((d, tf), lambda i, f: (0, f)), pl.BlockSpec((d, tf), lambda i, f: (0, nf + f)),
                pl.BlockSpec((CONV_W, tf), lambda i, f: (0, f)), pl.BlockSpec((CONV_W, tf), lambda i, f: (0, nf + f)),
                pl.BlockSpec((1, tf), lambda i, f: (0, f)), pl.BlockSpec((1, tf), lambda i, f: (0, nf + f)),
                pl.BlockSpec((tf, d), lambda i, f: (f, 0)),
                cst((1, d)), cst(w_pg.shape), cst(w_pp.shape), cst((1, d))]
    args = [x, pe, g_ffn, w_up, w_up, conv_w, conv_w, conv_b, conv_b, w_down, g_ple, w_pg, w_pp, g_final]
    if with_state:
        p1, p2 = state
        in_specs += [pl.BlockSpec((tb, tf), lambda i, f: (i, f)), pl.BlockSpec((tb, tf), lambda i, f: (i, nf + f)),
                     pl.BlockSpec((tb, tf), lambda i, f: (i, f)), pl.BlockSpec((tb, tf), lambda i, f: (i, nf + f))]
        args += [p1, p1, p2, p2]
        st_shape = jax.ShapeDtypeStruct((n, dff), F32)
        st_spec = pl.BlockSpec((tb, tf), lambda i, f: (i, f))
    else:
        st_shape = jax.ShapeDtypeStruct((nblk, CONV_W - 1, dff), F32)
        st_spec = pl.BlockSpec((None, CONV_W - 1, tf), lambda i, f: (i, 0, f))
    return pl.pallas_call(
        kern,
        grid=(nblk, nf),
        in_specs=in_specs,
        out_specs=[tok(d), st_spec, st_spec],
        out_shape=[jax.ShapeDtypeStruct((n, d), F32), st_shape, st_shape],
        scratch_shapes=[pltpu.VMEM((tb, d), BF), pltpu.VMEM((tb, d), F32),
                        pltpu.VMEM((nf, CONV_HALO, tf), F32), pltpu.VMEM((nf, CONV_HALO, tf), F32),
                        pltpu.VMEM((tb + CONV_HALO, tf), F32)],
        compiler_params=_cparams(("arbitrary", "arbitrary")),
        name="ffn",
    )(*args)


def _pick_block(n, target):
    b = min(n, target)
    while n % b:
        b //= 2
    return b


def kernel(x_prompt, x_sample, cache_fox_k, cache_fox_v, cache_fox_logf, cache_sb_k, cache_sb_v, state_pool, state_ffn_conv, page_table, p_prompt, p_sample, g_mix, w_in, b_f, gm_ln_g, gm_ln_b, gm_ws, gm_bs, pm_w, pm_scale, w_br, w_out, g_ffn, w_up, conv_w, conv_b, w_down, g_ple, w_ple_gate, w_ple_proj, g_final):
    nbp, t, d = x_prompt.shape
    nbs, ts, _ = x_sample.shape
    depth = w_in.shape[0]
    n_pool, page = cache_fox_k.shape[1], cache_fox_k.shape[2]
    n_pages = page_table.shape[1]
    past = n_pages * page
    dff = w_down.shape[1]
    f2 = 2 * dff
    off_mix = N_BRANCH * d
    off_f = off_mix + 6 * BR_WIDTH
    off_sbq = off_f + N_HEADS
    np_tok, ns_tok = nbp * t, nbs * ts

    tb_p = _pick_block(t, 512)
    tq = _pick_block(t, 256)
    tf = 256
    npp = _pick_block(n_pages, 8)

    row = lambda a: a.reshape(1, -1)
    xp = x_prompt.reshape(np_tok, d)
    xs = x_sample.reshape(ns_tok, d)
    kv_pages = lambda c: jnp.transpose(c, (0, 1, 3, 4, 2)).reshape(depth, n_pool, BR_WIDTH, page)
    ck_f, cv_f, ck_s, cv_s = kv_pages(cache_fox_k), kv_pages(cache_fox_v), kv_pages(cache_sb_k), kv_pages(cache_sb_v)
    clf = jnp.pad(jnp.swapaxes(cache_fox_logf, 2, 3), ((0, 0), (0, 0), (0, SUBLANES - N_HEADS), (0, 0)))

    gm_tile = lambda a: jnp.tile(a[:, :ts, :ts], (1, nbs, nbs))

    w_t = jnp.transpose(w_in, (2, 0, 1))
    blocks = lambda start, n: [start + W_BLOCK * j for j in range(n // W_BLOCK)]
    mix_starts = (blocks(off_mix, 3 * BR_WIDTH) + blocks(off_mix + 3 * BR_WIDTH, BR_WIDTH) + blocks(off_sbq, BR_WIDTH)
                  + [off_f]
                  + blocks(off_mix + 4 * BR_WIDTH, 2 * BR_WIDTH) + blocks(off_sbq + BR_WIDTH, 2 * BR_WIDTH))
    w_gate_all = _wprep(w_t, blocks(0, off_mix))
    w_mix_all = _wprep(w_t, mix_starts)

    new_p = [[] for _ in range(7)]
    new_s = [[] for _ in range(8)]
    for i in range(depth):
        bf_row = jnp.pad(b_f[i], (0, LANES - N_HEADS)).reshape(1, LANES)
        w_br_b = w_br[i].astype(BF)
        w_out_b = w_out[i].astype(BF)
        w_up_b = w_up[i].astype(BF)
        w_down_b = w_down[i].astype(BF)
        w_pg = w_ple_gate[i].astype(BF)
        w_pp = w_ple_proj[i].astype(BF)
        w_pm_bd = jax.scipy.linalg.block_diag(*[pm_w[i, g] for g in range(len(POOL_WINDOWS))]).astype(BF)
        bs_full = lambda rows: jnp.repeat(jnp.tile(gm_bs[i][:, :min(rows, GM_CHUNK)].T, (rows // min(rows, GM_CHUNK), 1)),
                                          HEAD_DIM, axis=1)
        common = (row(g_mix[i]), w_mix_all, bf_row, row(gm_ln_g[i]), row(gm_ln_b[i]))
        final = i == depth - 1
        heads = lambda a: jnp.transpose(a.reshape(a.shape[0], N_HEADS, HEAD_DIM, a.shape[2]), (0, 3, 1, 2))
        logf = lambda a: jnp.transpose(a[:, :N_HEADS], (0, 2, 1))

        gm_rows = min(GM_CHUNK, tb_p)
        (o_gm, pool_in, fq, sq, _, cum_c, lft, cum_r, fk, fv, sk, sv, fkb, fvb, skb, svb) = _inproj(
            xp, *common, gm_ws[i][:, :gm_rows, :gm_rows], bs_full(gm_rows),
            layer=i, tb=tb_p, seg=t, gm_rows=gm_rows, gm_chunk=GM_CHUNK, nseq=nbp)
        o_pool = _pool(pool_in, w_pm_bd, row(pm_scale[i]), tb=tb_p, rows_per_seq=t, offset=0, pos0=0)
        o_fox = _fox(fq, fkb, fvb, cum_c, cum_r, nseq=nbp, t=t, tq=tq)
        o_sb = _sb(sq, skb, svb, nseq=nbp, t=t, tq=tq)
        x1 = _merge(xp, o_gm, o_pool, o_fox, o_sb, row(g_mix[i]), w_gate_all, w_br_b, w_out_b, layer=i, tb=tb_p)
        xp, st_g, st_v = _ffn(x1, p_prompt[i].reshape(np_tok, -1), row(g_ffn[i]), w_up_b, conv_w[i], row(conv_b[i]),
                              w_down_b, row(g_ple[i]), w_pg, w_pp, row(g_final), None,
                              tb=tb_p, seg=t, tf=tf, final=final)
        new_p[0].append(heads(fk))
        new_p[1].append(heads(fv))
        new_p[2].append(logf(lft))
        new_p[3].append(heads(sk))
        new_p[4].append(heads(sv))
        new_p[5].append(pool_in.reshape(nbp, t, BR_WIDTH)[:, t - POOL_BUF:])
        last = slice(t // tb_p - 1, None, t // tb_p)
        new_p[6].append(jnp.concatenate([st_g[last], st_v[last]], axis=-1))

        (o_gm, pool_in, fq, sq, vn, _, lft, _, fk, fv, sk, sv, _, _, _, _) = _inproj(
            xs, *common, gm_tile(gm_ws[i]), bs_full_sample(gm_bs[i], ts, nbs),
            layer=i, tb=ns_tok, seg=ts, gm_rows=ns_tok, gm_chunk=ts, nseq=1)
        grp_rows = -(-(POOL_BUF + ts) // SUBLANES) * SUBLANES
        lead = grp_rows - POOL_BUF - ts
        full = jnp.concatenate([jnp.zeros((nbs, lead, BR_WIDTH), F32), state_pool[i],
                                pool_in.reshape(nbs, ts, BR_WIDTH)], axis=1)
        o_pool = _pool(full.reshape(nbs * grp_rows, BR_WIDTH), w_pm_bd, row(pm_scale[i]),
                       tb=nbs * grp_rows, rows_per_seq=grp_rows, offset=grp_rows - ts, pos0=past)
        o_pool = o_pool.reshape(nbs, grp_rows, BR_WIDTH)[:, grp_rows - ts:].reshape(ns_tok, BR_WIDTH)
        o_fox, o_sb = _dec_attn(
            page_table, _block_diag_queries(fq, nbs, ts), _block_diag_queries(sq, nbs, ts),
            _pad_new(fk, nbs, ts), _pad_new(fv, nbs, ts),
            jnp.pad(jnp.swapaxes(lft.reshape(SUBLANES, nbs, ts), 0, 1), ((0, 0), (0, 0), (0, LANES - ts))),
            _pad_new(sk, nbs, ts), _pad_new(sv, nbs, ts),
            ck_f, cv_f, clf, ck_s, cv_s, layer=i, npp=npp, tnew=ts)
        o_fox = o_fox.reshape(ns_tok, BR_WIDTH).astype(BF)
        o_sb = o_sb.reshape(ns_tok, BR_WIDTH).astype(BF)
        x1 = _merge(xs, o_gm, o_pool, o_fox, o_sb, row(g_mix[i]), w_gate_all, w_br_b, w_out_b, layer=i, tb=ns_tok)
        buf = state_ffn_conv[i]
        zero = jnp.zeros((nbs, ts - 1, f2), F32)
        p1 = jnp.concatenate([buf[:, 1:2], zero], axis=1).reshape(ns_tok, f2)
        p2 = jnp.concatenate([buf, zero[:, 1:]], axis=1).reshape(ns_tok, f2)
        xs, a_g, a_v = _ffn(x1, p_sample[i].reshape(ns_tok, -1), row(g_ffn[i]), w_up_b, conv_w[i], row(conv_b[i]),
                            w_down_b, row(g_ple[i]), w_pg, w_pp, row(g_final), (p1, p2),
                            tb=ns_tok, seg=ts, tf=tf, final=final)
        new_s[0].append(heads(fk).reshape(nbs, ts, N_HEADS, HEAD_DIM))
        new_s[1].append(heads(fv).reshape(nbs, ts, N_HEADS, HEAD_DIM))
        new_s[2].append(logf(lft).reshape(nbs, ts, N_HEADS))
        new_s[3].append(heads(sk).reshape(nbs, ts, N_HEADS, HEAD_DIM))
        new_s[4].append(heads(sv).reshape(nbs, ts, N_HEADS, HEAD_DIM))
        new_s[5].append(jnp.concatenate([state_pool[i], pool_in.reshape(nbs, ts, BR_WIDTH)], axis=1)[:, ts:])
        a_full = jnp.concatenate([a_g, a_v], axis=-1).reshape(nbs, ts, f2)
        new_s[6].append(jnp.concatenate([buf, a_full], axis=1)[:, ts:])
        new_s[7].append(vn.reshape(nbs, ts, BR_WIDTH))

    stk = lambda lst: jnp.stack(lst, axis=0)
    return (xp.reshape(nbp, t, d), xs.reshape(nbs, ts, d),
            *[stk(a) for a in new_p], *[stk(a) for a in new_s])


def bs_full_sample(gm_bs_i, ts, nbs):
    return jnp.repeat(jnp.tile(gm_bs_i[:, :ts].T, (nbs, 1)), HEAD_DIM, axis=1)


def _block_diag_queries(q, nbs, ts):
    q = q.reshape(nbs, ts, 1, N_HEADS, HEAD_DIM)
    eye = jnp.eye(SUBLANES, N_HEADS, dtype=q.dtype).reshape(1, 1, SUBLANES, N_HEADS, 1)
    return (q * eye).reshape(nbs, ts * SUBLANES, BR_WIDTH)


def _pad_new(a, nbs, ts):
    return jnp.pad(jnp.swapaxes(a.reshape(BR_WIDTH, nbs, ts), 0, 1), ((0, 0), (0, 0), (0, LANES - ts)))
```

```python
import functools

import jax
import jax.numpy as jnp
from jax import lax
from jax.experimental import pallas as pl
from jax.experimental.pallas import tpu as pltpu

F32 = jnp.float32
BF = jnp.bfloat16

EPS = 1e-6
N_BRANCH = 4
BR_WIDTH = 256
N_HEADS = 4
HEAD_DIM = 64
LANES = 128
SUBLANES = 8
POOL_WINDOWS = (2, 4, 8, 16)
POOL_BUF = 15
POOL_HALO = 16
CONV_W = 3
GM_CHUNK = 128
NEG_BIG = -1e30
EXP_ZERO_BELOW = -104.0
VMEM_LIMIT = 48 * 1024 * 1024

C_U, C_V, C_POOL, C_FQ, C_SQ, C_F = [BR_WIDTH * i for i in range(6)]
N_TOKMAJ = C_F + LANES
R_FK, R_FV, R_SK, R_SV = [BR_WIDTH * i for i in range(4)]
N_HEADMAJ = 4 * BR_WIDTH
W_BLOCK = 128


def _cparams(sem):
    return pltpu.CompilerParams(dimension_semantics=sem, vmem_limit_bytes=VMEM_LIMIT)


def _rms(x, g):
    return x * lax.rsqrt(jnp.mean(x * x, axis=-1, keepdims=True) + EPS) * g


def _softplus(x):
    return jnp.maximum(x, 0.0) + jnp.log1p(jnp.exp(-jnp.abs(x)))


def _split3(x):
    hi = x.astype(BF)
    r = x - hi.astype(F32)
    mid = r.astype(BF)
    r = r - mid.astype(F32)
    return hi, mid, r.astype(BF)


def _dot(a, b):
    return jnp.dot(a, b, preferred_element_type=F32)


def _dot_nt(a, b):
    return lax.dot_general(a, b, (((1,), (1,)), ((), ())), preferred_element_type=F32)


def _dot_exact(x, ones_bf, left, terms=3):
    parts = _split3(x)[:terms]
    out = None
    for p in parts:
        d = _dot(ones_bf, p) if left else _dot(p, ones_bf)
        out = d if out is None else out + d
    return out


def _const_spec(shape):
    nd = len(shape)
    return pl.BlockSpec(shape, lambda *_: (0,) * nd)


def _wprep_kernel(st_ref, w_ref, o_ref):
    del st_ref
    for layer in range(o_ref.shape[0]):
        o_ref[layer] = w_ref[:, layer, :].astype(BF)


def _wprep(w_t, starts):
    n_out, depth, d = w_t.shape
    nblk = len(starts)
    grid_spec = pltpu.PrefetchScalarGridSpec(
        num_scalar_prefetch=1,
        grid=(nblk,),
        in_specs=[pl.BlockSpec((pl.Element(W_BLOCK), pl.Element(depth), pl.Element(d)), lambda j, st: (st[j], 0, 0))],
        out_specs=pl.BlockSpec((depth, W_BLOCK, d), lambda j, st: (0, j, 0)),
    )
    return pl.pallas_call(
        _wprep_kernel,
        grid_spec=grid_spec,
        out_shape=jax.ShapeDtypeStruct((depth, nblk * W_BLOCK, d), BF),
        compiler_params=_cparams(("arbitrary",)),
        name="wprep",
    )(jnp.asarray(starts, jnp.int32), w_t)


def _inproj_kernel(x_ref, g_ref, w_ref, bfr_ref, lng_ref, lnb_ref, ws_ref, bs_ref,
                   ogm_ref, pool_ref, fq_ref, sq_ref, vn_ref, cc_ref, lft_ref, cr_ref,
                   fk_ref, fv_ref, sk_ref, sv_ref, fkb_ref, fvb_ref, skb_ref, svb_ref,
                   carc_ref, *, tb, seg, gm_rows, gm_chunk):
    i = pl.program_id(0)
    h = _rms(x_ref[...], g_ref[...]).astype(BF)
    z = _dot_nt(h, w_ref[0:N_TOKMAJ, :])
    zt = _dot_nt(w_ref[N_TOKMAJ:N_TOKMAJ + N_HEADMAJ, :], h)

    r = lax.broadcasted_iota(jnp.int32, (gm_rows, gm_rows), 0)
    c = lax.broadcasted_iota(jnp.int32, (gm_rows, gm_rows), 1)
    mix_mask = (r >= c) & ((r // gm_chunk) == (c // gm_chunk))
    grp = lax.broadcasted_iota(jnp.int32, (gm_rows, BR_WIDTH), 1) // HEAD_DIM
    for blk in range(tb // gm_rows):
        rows = slice(blk * gm_rows, (blk + 1) * gm_rows)
        u = z[rows, C_U:C_U + BR_WIDTH]
        v = z[rows, C_V:C_V + BR_WIDTH]
        mu = jnp.mean(v, axis=-1, keepdims=True)
        var = jnp.mean(jnp.square(v - mu), axis=-1, keepdims=True)
        vn = (v - mu) * lax.rsqrt(var + EPS) * lng_ref[...] + lnb_ref[...]
        vn_ref[rows, :] = vn
        vnb = vn.astype(BF)
        s = bs_ref[...]
        for g in range(N_HEADS):
            wg = jnp.where(mix_mask, ws_ref[g], 0.0).astype(BF)
            s = s + jnp.where(grp == g, _dot(wg, vnb), 0.0)
        ogm_ref[rows, :] = (u * s).astype(BF)

    pool_ref[...] = z[:, C_POOL:C_POOL + BR_WIDTH]
    scale = HEAD_DIM ** -0.5
    fq_ref[...] = (z[:, C_FQ:C_FQ + BR_WIDTH] * scale).astype(BF)
    sq_ref[...] = (z[:, C_SQ:C_SQ + BR_WIDTH] * scale).astype(BF)
    for start, full_ref, half_ref in ((R_FK, fk_ref, fkb_ref), (R_FV, fv_ref, fvb_ref),
                                      (R_SK, sk_ref, skb_ref), (R_SV, sv_ref, svb_ref)):
        part = zt[start:start + BR_WIDTH, :]
        full_ref[...] = part
        half_ref[...] = part.astype(BF)

    lf = -_softplus(-(z[:, C_F:C_F + LANES] + bfr_ref[...]))
    lft_ref[...] = lf.T[0:SUBLANES, :]
    r = lax.broadcasted_iota(jnp.int32, (tb, tb), 0)
    c = lax.broadcasted_iota(jnp.int32, (tb, tb), 1)
    if seg >= tb:
        lower = (r >= c)
        nblk_seq = seg // tb

        @pl.when(i % nblk_seq == 0)
        def _():
            carc_ref[...] = jnp.zeros_like(carc_ref)
    else:
        lower = (r >= c) & ((r // seg) == (c // seg))
        carc_ref[...] = jnp.zeros_like(carc_ref)
    cum_c = _dot_exact(lf, jnp.where(lower, 1.0, 0.0).astype(BF), left=True) + carc_ref[0:1, :]
    cc_ref[...] = cum_c
    cr_ref[...] = cum_c.T[0:SUBLANES, :]
    carc_ref[...] = jnp.broadcast_to(cum_c[tb - 1:tb, :], carc_ref.shape)


def _inproj(x, g_mix, w_mix, bf_row, ln_g, ln_b, ws, bs, *, layer, tb, seg, gm_rows, gm_chunk, nseq):
    n, d = x.shape
    kern = functools.partial(_inproj_kernel, tb=tb, seg=seg, gm_rows=gm_rows, gm_chunk=gm_chunk)
    tok = lambda w: pl.BlockSpec((tb, w), lambda i: (i, 0))
    per_seq = n // nseq
    nblk_seq = per_seq // tb
    chan = lambda rows: pl.BlockSpec((None, rows, tb), lambda i: (i // nblk_seq, 0, i % nblk_seq))
    tokmaj = lambda width, dt: jax.ShapeDtypeStruct((n, width), dt)
    chanmaj = lambda rows, dt: jax.ShapeDtypeStruct((nseq, rows, per_seq), dt)
    out_shape = [
        tokmaj(BR_WIDTH, BF),
        tokmaj(BR_WIDTH, F32),
        tokmaj(BR_WIDTH, BF),
        tokmaj(BR_WIDTH, BF),
        tokmaj(BR_WIDTH, F32),
        tokmaj(LANES, F32),
        chanmaj(SUBLANES, F32),
        chanmaj(SUBLANES, F32),
    ] + [chanmaj(BR_WIDTH, F32)] * 4 + [chanmaj(BR_WIDTH, BF)] * 4
    out_specs = ([tok(BR_WIDTH)] * 5 + [tok(LANES), chan(SUBLANES), chan(SUBLANES)] + [chan(BR_WIDTH)] * 8)
    w_rows = w_mix.shape[1]
    return pl.pallas_call(
        kern,
        grid=(n // tb,),
        in_specs=[tok(d), _const_spec((1, d)),
                  pl.BlockSpec((None, w_rows, d), lambda i: (layer, 0, 0), pipeline_mode=pl.Buffered(1)),
                  _const_spec(bf_row.shape), _const_spec(ln_g.shape),
                  _const_spec(ln_b.shape), _const_spec(ws.shape), _const_spec(bs.shape)],
        out_specs=out_specs,
        out_shape=out_shape,
        scratch_shapes=[pltpu.VMEM((SUBLANES, LANES), F32)],
        compiler_params=_cparams(("arbitrary",)),
        name="inproj",
    )(x, g_mix, w_mix, bf_row, ln_g, ln_b, ws, bs)


def _pool_kernel(x_ref, w_ref, sc_ref, o_ref, ext_ref, *, tb, rows_per_seq, offset, pos0, nblk_seq):
    i = pl.program_id(0)

    @pl.when(i % nblk_seq == 0)
    def _():
        ext_ref[0:POOL_HALO, :] = jnp.zeros((POOL_HALO, BR_WIDTH), F32)

    ext_ref[POOL_HALO:POOL_HALO + tb, :] = x_ref[...]
    rowi = lax.broadcasted_iota(jnp.int32, (tb, LANES), 0)
    lane = lax.broadcasted_iota(jnp.int32, (tb, LANES), 1)
    pos = pos0 + (i * tb + rowi) % rows_per_seq - offset
    first = lane < HEAD_DIM
    halves = []
    for half in range(2):
        cols = slice(half * LANES, (half + 1) * LANES)
        shifted = lambda j: ext_ref[POOL_HALO - j:POOL_HALO - j + tb, cols]
        w_small, w_big = POOL_WINDOWS[2 * half], POOL_WINDOWS[2 * half + 1]
        cur = shifted(0)
        acc = cur
        for j in range(1, w_small):
            acc = acc + shifted(j)
        small = acc
        for j in range(w_small, w_big):
            acc = acc + shifted(j)
        win = jnp.where(first, small, acc)
        width = jnp.where(first, w_small, w_big)
        cnt = jnp.clip(pos + 1, 1, width).astype(F32)
        halves.append(win / cnt - cur)
    d = jnp.concatenate(halves, axis=1).astype(BF)
    o_ref[...] = (_dot(d, w_ref[...]) * sc_ref[...]).astype(BF)
    ext_ref[0:POOL_HALO, :] = ext_ref[tb:tb + POOL_HALO, :]


def _pool(x, w_bd, scale, *, tb, rows_per_seq, offset, pos0):
    n = x.shape[0]
    kern = functools.partial(_pool_kernel, tb=tb, rows_per_seq=rows_per_seq, offset=offset, pos0=pos0,
                             nblk_seq=max(1, rows_per_seq // tb))
    return pl.pallas_call(
        kern,
        grid=(n // tb,),
        in_specs=[pl.BlockSpec((tb, BR_WIDTH), lambda i: (i, 0)), _const_spec(w_bd.shape), _const_spec(scale.shape)],
        out_specs=pl.BlockSpec((tb, BR_WIDTH), lambda i: (i, 0)),
        out_shape=jax.ShapeDtypeStruct((n, BR_WIDTH), BF),
        scratch_shapes=[pltpu.VMEM((tb + POOL_HALO, BR_WIDTH), F32)],
        compiler_params=_cparams(("arbitrary",)),
        name="pool",
    )(x, w_bd, scale)


def _head_queries(q):
    left = lax.broadcasted_iota(jnp.int32, (q.shape[0], LANES), 1) < HEAD_DIM
    out = []
    for h in range(N_HEADS):
        pair = q[:, (h // 2) * LANES:(h // 2 + 1) * LANES]
        keep = left if h % 2 == 0 else jnp.logical_not(left)
        out.append(jnp.where(keep, pair, jnp.zeros_like(pair)))
    return out, left


def _fox_kernel(q_ref, k_ref, v_ref, cc_ref, cr_ref, o_ref, kn_ref, *, tq):
    qb = pl.program_id(1)
    qh, left = _head_queries(q_ref[...])
    cc = cc_ref[...]
    cq = [cc[:, h:h + 1] for h in range(N_HEADS)]
    r = lax.broadcasted_iota(jnp.int32, (tq, tq), 0)
    c = lax.broadcasted_iota(jnp.int32, (tq, tq), 1)
    causal = c <= r

    @pl.when(qb == 0)
    def _():
        hr = lax.broadcasted_iota(jnp.int32, (SUBLANES, BR_WIDTH), 0)
        hc = lax.broadcasted_iota(jnp.int32, (SUBLANES, BR_WIDTH), 1) // HEAD_DIM
        head_rows = jnp.where(hr == hc, 1.0, 0.0).astype(BF)

        def chunk(ci, best):
            kc = k_ref[:, pl.ds(pl.multiple_of(ci * tq, tq), tq)].astype(F32)
            n2 = _dot_exact(kc * kc, head_rows, left=True)
            return jnp.maximum(best, jnp.max(n2, axis=1, keepdims=True))

        best = lax.fori_loop(0, k_ref.shape[1] // tq, chunk, jnp.zeros((SUBLANES, 1), F32))
        kn_ref[...] = jnp.broadcast_to(best, kn_ref.shape)

    reach = []
    for h in range(N_HEADS):
        qf = qh[h].astype(F32)
        qn2 = jnp.sum(qf * qf, axis=-1, keepdims=True)
        reach.append(jnp.sqrt(qn2 * kn_ref[h:h + 1, 0:1]) * 1.001 + 0.01 + cq[h])

    def block(j, carry, masked):
        ms, ls, accs = carry
        start = pl.multiple_of(j * tq, tq)
        kblk = k_ref[:, pl.ds(start, tq)]
        vblk = v_ref[:, pl.ds(start, tq)]
        heads = range(N_HEADS)
        pairs = [slice((h // 2) * LANES, (h // 2 + 1) * LANES) for h in heads]
        ss = [_dot(qh[h], kblk[pairs[h], :]) + (cq[h] - cr_ref[h:h + 1, pl.ds(start, tq)]) for h in heads]
        if masked:
            ss = [jnp.where(causal, s, NEG_BIG) for s in ss]
        new_m = [jnp.maximum(ms[h], jnp.max(ss[h], axis=-1, keepdims=True)) for h in heads]
        alphas = [jnp.exp(ms[h] - new_m[h]) for h in heads]
        ps = [jnp.exp(ss[h] - new_m[h]) for h in heads]
        new_l = [ls[h] * alphas[h] + jnp.sum(ps[h], axis=-1, keepdims=True) for h in heads]
        pv = [_dot_nt(ps[h].astype(BF), vblk[pairs[h], :]) for h in heads]
        new_acc = []
        for hp in range(N_HEADS // 2):
            a = jnp.where(left, alphas[2 * hp], alphas[2 * hp + 1])
            new_acc.append(accs[hp] * a + jnp.where(left, pv[2 * hp], pv[2 * hp + 1]))
        return tuple(new_m), tuple(new_l), tuple(new_acc)

    init = (tuple(jnp.full((tq, 1), NEG_BIG, F32) for _ in range(N_HEADS)),
            tuple(jnp.zeros((tq, 1), F32) for _ in range(N_HEADS)),
            tuple(jnp.zeros((tq, LANES), F32) for _ in range(N_HEADS // 2)))
    def live(state):
        j, ms = state[0], state[1]
        start = pl.multiple_of(jnp.maximum(j, 0) * tq, tq)
        gap = None
        for h in range(N_HEADS):
            ck_last = cr_ref[h:h + 1, pl.ds(start, tq)][:, tq - 1:tq]
            g = reach[h] - ck_last - ms[h]
            gap = g if gap is None else jnp.maximum(gap, g)
        return (j >= 0) & (jnp.max(gap) > EXP_ZERO_BELOW)

    def step(state):
        j = state[0]
        return (j - 1,) + block(j, state[1:], False)

    first = block(qb, init, True)
    _, ms, ls, accs = lax.while_loop(live, step, (qb - 1,) + first)
    for hp in range(N_HEADS // 2):
        den = jnp.where(left, ls[2 * hp], ls[2 * hp + 1])
        o_ref[:, hp * LANES:(hp + 1) * LANES] = (accs[hp] / den).astype(BF)


def _fox(q, kb, vb, cum_c, cum_r, *, nseq, t, tq):
    kern = functools.partial(_fox_kernel, tq=tq)
    nq = t // tq
    return pl.pallas_call(
        kern,
        grid=(nseq, nq),
        in_specs=[pl.BlockSpec((tq, BR_WIDTH), lambda n, i: (n * nq + i, 0)),
                  pl.BlockSpec((None, BR_WIDTH, t), lambda n, i: (n, 0, 0)),
                  pl.BlockSpec((None, BR_WIDTH, t), lambda n, i: (n, 0, 0)),
                  pl.BlockSpec((tq, LANES), lambda n, i: (n * nq + i, 0)),
                  pl.BlockSpec((None, SUBLANES, t), lambda n, i: (n, 0, 0))],
        out_specs=pl.BlockSpec((tq, BR_WIDTH), lambda n, i: (n * nq + i, 0)),
        out_shape=jax.ShapeDtypeStruct((nseq * t, BR_WIDTH), BF),
        scratch_shapes=[pltpu.VMEM((SUBLANES, LANES), F32)],
        compiler_params=_cparams(("arbitrary", "arbitrary")),
        name="fox_prompt",
    )(q, kb, vb, cum_c, cum_r)


def _sb_kernel(q_ref, k_ref, v_ref, o_ref, *, tq):
    qb = pl.program_id(1)
    qh, left = _head_queries(q_ref[...])
    r = lax.broadcasted_iota(jnp.int32, (tq, tq), 0)
    c = lax.broadcasted_iota(jnp.int32, (tq, tq), 1)
    strict = c < r
    after = jnp.where(r > c, 1.0, 0.0).astype(BF)

    def block(j, carry, masked):
        cs, accs = carry
        start = pl.multiple_of(j * tq, tq)
        kblk = k_ref[:, pl.ds(start, tq)]
        vblk = v_ref[:, pl.ds(start, tq)]
        pairs = [slice((h // 2) * LANES, (h // 2 + 1) * LANES) for h in range(N_HEADS)]
        zs = [_dot(qh[h], kblk[pairs[h], :]) for h in range(N_HEADS)]
        lrests = [-_softplus(z) for z in zs]
        if masked:
            lrests = [jnp.where(strict, l, 0.0) for l in lrests]
        laters = [_dot_exact(lrests[h], after, left=False, terms=2) + cs[h] for h in range(N_HEADS)]
        weights = [jnp.exp(zs[h] + lrests[h] + laters[h]) for h in range(N_HEADS)]
        if masked:
            weights = [jnp.where(strict, a, 0.0) for a in weights]
        av = [_dot_nt(weights[h].astype(BF), vblk[pairs[h], :]) for h in range(N_HEADS)]
        new_c = [cs[h] + jnp.sum(lrests[h], axis=-1, keepdims=True) for h in range(N_HEADS)]
        new_acc = tuple(accs[hp] + jnp.where(left, av[2 * hp], av[2 * hp + 1]) for hp in range(N_HEADS // 2))
        return tuple(new_c), new_acc

    init = (tuple(jnp.zeros((tq, 1), F32) for _ in range(N_HEADS)),
            tuple(jnp.zeros((tq, LANES), F32) for _ in range(N_HEADS // 2)))
    def live(state):
        j, cs = state[0], state[1]
        top = cs[0]
        for h in range(1, N_HEADS):
            top = jnp.maximum(top, cs[h])
        return (j >= 0) & (jnp.max(top) > EXP_ZERO_BELOW)

    def step(state):
        j = state[0]
        return (j - 1,) + block(j, state[1:], False)

    first = block(qb, init, True)
    _, _, accs = lax.while_loop(live, step, (qb - 1,) + first)
    for hp in range(N_HEADS // 2):
        o_ref[:, hp * LANES:(hp + 1) * LANES] = accs[hp].astype(BF)


def _sb(q, kb, vb, *, nseq, t, tq):
    kern = functools.partial(_sb_kernel, tq=tq)
    nq = t // tq
    return pl.pallas_call(
        kern,
        grid=(nseq, nq),
        in_specs=[pl.BlockSpec((tq, BR_WIDTH), lambda n, i: (n * nq + i, 0)),
                  pl.BlockSpec((None, BR_WIDTH, t), lambda n, i: (n, 0, 0)),
                  pl.BlockSpec((None, BR_WIDTH, t), lambda n, i: (n, 0, 0))],
        out_specs=pl.BlockSpec((tq, BR_WIDTH), lambda n, i: (n * nq + i, 0)),
        out_shape=jax.ShapeDtypeStruct((nseq * t, BR_WIDTH), BF),
        compiler_params=_cparams(("arbitrary", "arbitrary")),
        name="sb_prompt",
    )(q, kb, vb)


def _dec_attn_kernel(pt_ref, qf_ref, qs_ref, knf_ref, vnf_ref, lfn_ref, kns_ref, vns_ref, *rest, npp, tnew):
    del pt_ref
    DEC_ROWS = tnew * SUBLANES
    pages = rest[:5 * npp]
    of_ref, os_ref = rest[5 * npp:5 * npp + 2]
    m_ref, l_ref, accf_ref, offf_ref, e_ref, cs_ref, accs_ref = rest[5 * npp + 2:]
    s_id = pl.program_id(1)
    n_steps = pl.num_programs(1)

    r = lax.broadcasted_iota(jnp.int32, (LANES, LANES), 0)
    c = lax.broadcasted_iota(jnp.int32, (LANES, LANES), 1)
    after = jnp.where(r > c, 1.0, 0.0).astype(BF)
    rowi = lax.broadcasted_iota(jnp.int32, (DEC_ROWS, LANES), 0)
    lane = lax.broadcasted_iota(jnp.int32, (DEC_ROWS, LANES), 1)
    t_row = rowi // SUBLANES

    def suffix(l_list):
        nb = len(l_list)
        stack = l_list[0] if nb == 1 else jnp.concatenate(l_list, axis=0)
        inner = _dot_exact(stack, after, left=False)
        out = [None] * nb
        off = jnp.zeros((DEC_ROWS, 1), F32)
        for b in range(nb - 1, -1, -1):
            out[b] = inner[b * DEC_ROWS:(b + 1) * DEC_ROWS] + off
            off = off + jnp.sum(l_list[b], axis=-1, keepdims=True)
        return out, off

    def fox_part(kf, vf, lf8, is_new):
        nb = len(kf)
        qf = qf_ref[...]
        lf = [jnp.concatenate([x] * (DEC_ROWS // SUBLANES), axis=0) for x in lf8]
        rsum, tot = suffix(lf)
        base = offf_ref[...] - e_ref[...]
        s_list = []
        for b in range(nb):
            s = _dot(qf, kf[b].astype(BF)) + rsum[b] + base
            if is_new:
                s = jnp.where((lane <= t_row) & (lane < tnew), s, NEG_BIG)
            s_list.append(s)
        m_old = m_ref[...]
        m_blk = s_list[0]
        for b in range(1, nb):
            m_blk = jnp.maximum(m_blk, s_list[b])
        m_new = jnp.maximum(m_old, jnp.max(m_blk, axis=-1, keepdims=True))
        alpha = jnp.exp(m_old - m_new)
        psum = jnp.zeros((DEC_ROWS, LANES), F32)
        pv = jnp.zeros((DEC_ROWS, BR_WIDTH), F32)
        for b in range(nb):
            p = jnp.exp(s_list[b] - m_new)
            psum = psum + p
            pv = pv + _dot_nt(p.astype(BF), vf[b].astype(BF))
        m_ref[...] = m_new
        l_ref[...] = l_ref[...] * alpha + jnp.sum(psum, axis=-1, keepdims=True)
        accf_ref[...] = accf_ref[...] * alpha[:, 0:1] + pv
        offf_ref[...] = offf_ref[...] + tot

    def sb_part(ks, vs, is_new):
        nb = len(ks)
        qs = qs_ref[...]
        z_list, l_list = [], []
        for b in range(nb):
            z = _dot(qs, ks[b].astype(BF))
            lrest = -_softplus(z)
            if is_new:
                lrest = jnp.where((lane < t_row) & (lane < tnew), lrest, 0.0)
            z_list.append(z)
            l_list.append(lrest)
        later, tot_s = suffix(l_list)
        cs = cs_ref[...]
        av = jnp.zeros((DEC_ROWS, BR_WIDTH), F32)
        for b in range(nb):
            a = jnp.exp(z_list[b] + l_list[b] + later[b] + cs)
            if is_new:
                a = jnp.where((lane < t_row) & (lane < tnew), a, 0.0)
            av = av + _dot_nt(a.astype(BF), vs[b].astype(BF))
        accs_ref[...] = accs_ref[...] + av
        cs_ref[...] = cs + tot_s

    @pl.when(s_id == 0)
    def _():
        m_ref[...] = jnp.full(m_ref.shape, NEG_BIG, F32)
        l_ref[...] = jnp.zeros_like(l_ref)
        accf_ref[...] = jnp.zeros_like(accf_ref)
        offf_ref[...] = jnp.zeros_like(offf_ref)
        cs_ref[...] = jnp.zeros_like(cs_ref)
        accs_ref[...] = jnp.zeros_like(accs_ref)
        lfn = jnp.concatenate([lfn_ref[...]] * (DEC_ROWS // SUBLANES), axis=0)
        inner = _dot_exact(lfn, after, left=False)
        e = jnp.sum(jnp.where(lane == t_row, inner, 0.0), axis=-1, keepdims=True)
        e_ref[...] = jnp.broadcast_to(e, e_ref.shape)
        fox_part([knf_ref[...]], [vnf_ref[...]], [lfn_ref[...]], True)
        sb_part([kns_ref[...]], [vns_ref[...]], True)

    get = lambda k: [pages[5 * j + k][...] for j in range(npp)]

    @pl.when(s_id > 0)
    def _():
        fox_part(get(0), get(1), get(2), False)

    @pl.when((s_id > 0) & (jnp.max(cs_ref[...]) > EXP_ZERO_BELOW))
    def _():
        sb_part(get(3), get(4), False)

    @pl.when(s_id == n_steps - 1)
    def _():
        own = (lax.broadcasted_iota(jnp.int32, (DEC_ROWS, BR_WIDTH), 1) // HEAD_DIM
               == lax.broadcasted_iota(jnp.int32, (DEC_ROWS, BR_WIDTH), 0) % SUBLANES)
        nt = DEC_ROWS // SUBLANES
        fo = jnp.where(own, accf_ref[...] / l_ref[:, 0:1], 0.0).reshape(nt, SUBLANES, BR_WIDTH)
        so = jnp.where(own, accs_ref[...], 0.0).reshape(nt, SUBLANES, BR_WIDTH)
        of_ref[...] = jnp.sum(fo, axis=1)
        os_ref[...] = jnp.sum(so, axis=1)


def _dec_attn(page_table, qf, qs, knf, vnf, lfn, kns, vns, ck_f, cv_f, clf, ck_s, cv_s, *, layer, npp, tnew):
    nb, n_pages = page_table.shape
    page = ck_f.shape[3]
    DEC_ROWS = tnew * SUBLANES
    n_steps = 1 + n_pages // npp
    kern = functools.partial(_dec_attn_kernel, npp=npp, tnew=tnew)
    per_b = lambda shape: pl.BlockSpec((None,) + shape, lambda b, s, pt: (b, 0, 0))

    def page_spec(j, rows, width):
        def idx(b, s, pt):
            return (layer, pt[b, n_pages - jnp.maximum(s, 1) * npp + j], 0, 0)
        return pl.BlockSpec((None, None, rows, width), idx)

    in_specs = [per_b((DEC_ROWS, BR_WIDTH)), per_b((DEC_ROWS, BR_WIDTH)),
                per_b((BR_WIDTH, LANES)), per_b((BR_WIDTH, LANES)), per_b((SUBLANES, LANES)),
                per_b((BR_WIDTH, LANES)), per_b((BR_WIDTH, LANES))]
    args = [qf, qs, knf, vnf, lfn, kns, vns]
    for j in range(npp):
        in_specs += [page_spec(j, BR_WIDTH, page), page_spec(j, BR_WIDTH, page), page_spec(j, SUBLANES, page),
                     page_spec(j, BR_WIDTH, page), page_spec(j, BR_WIDTH, page)]
        args += [ck_f, cv_f, clf, ck_s, cv_s]
    nt = DEC_ROWS // SUBLANES
    out_spec = pl.BlockSpec((None, nt, BR_WIDTH), lambda b, s, pt: (b, 0, 0))
    wide = pltpu.VMEM((DEC_ROWS, BR_WIDTH), F32)
    narrow = pltpu.VMEM((DEC_ROWS, LANES), F32)
    grid_spec = pltpu.PrefetchScalarGridSpec(
        num_scalar_prefetch=1,
        grid=(nb, n_steps),
        in_specs=in_specs,
        out_specs=[out_spec, out_spec],
        scratch_shapes=[narrow, narrow, wide, narrow, narrow, narrow, wide],
    )
    return pl.pallas_call(
        kern,
        grid_spec=grid_spec,
        out_shape=[jax.ShapeDtypeStruct((nb, nt, BR_WIDTH), F32)] * 2,
        compiler_params=_cparams(("arbitrary", "arbitrary")),
        name="dec_attn",
    )(page_table, *args)


def _merge_kernel(x_ref, og_ref, op_ref, of_ref, os_ref, g_ref, wg_ref, wb_ref, wo_ref, o_ref):
    x = x_ref[...]
    d = x.shape[1]
    h = _rms(x, g_ref[...]).astype(BF)
    acc = jnp.zeros(x.shape, F32)
    for b, ref in enumerate((og_ref, op_ref, of_ref, os_ref)):
        gate = jax.nn.sigmoid(_dot_nt(h, wg_ref[b * d:(b + 1) * d, :]))
        acc = acc + gate * _dot(ref[...], wb_ref[b])
    o_ref[...] = x + _dot(acc.astype(BF), wo_ref[...])


def _resident(shape):
    nd = len(shape)
    return pl.BlockSpec(shape, lambda *_: (0,) * nd, pipeline_mode=pl.Buffered(1))


def _merge(x, o_gm, o_pool, o_fox, o_sb, g_mix, w_gate, w_br, w_out, *, layer, tb):
    n, d = x.shape
    tok = lambda w: pl.BlockSpec((tb, w), lambda i: (i, 0))
    gate_spec = pl.BlockSpec((None,) + w_gate.shape[1:], lambda i: (layer, 0, 0), pipeline_mode=pl.Buffered(1))
    return pl.pallas_call(
        _merge_kernel,
        grid=(n // tb,),
        in_specs=[tok(d), tok(BR_WIDTH), tok(BR_WIDTH), tok(BR_WIDTH), tok(BR_WIDTH), _const_spec((1, d)),
                  gate_spec, _resident(w_br.shape), _resident(w_out.shape)],
        out_specs=tok(d),
        out_shape=jax.ShapeDtypeStruct((n, d), F32),
        compiler_params=_cparams(("arbitrary",)),
        name="merge",
    )(x, o_gm, o_pool, o_fox, o_sb, g_mix, w_gate, w_br, w_out)


CONV_HALO = 8


def _gelu_tanh(x):
    return 0.5 * x * (1.0 + jnp.tanh(0.7978845608028654 * (x + 0.044715 * (x * x * x))))


def _ffn_kernel(*refs, tb, seg, nblk_seq, final, with_state, row_chunks):
    (x_ref, pe_ref, gf_ref, wug_ref, wuv_ref, cwg_ref, cwv_ref, cbg_ref, cbv_ref, wd_ref,
     gp_ref, wpg_ref, wpp_ref, gfin_ref) = refs[:14]
    refs = refs[14:]
    if with_state:
        p1g_ref, p1v_ref, p2g_ref, p2v_ref = refs[:4]
        refs = refs[4:]
    o_ref, sg_ref, sv_ref, h_ref, acc_ref, cg_ref, cv_ref = refs
    i = pl.program_id(0)
    f = pl.program_id(1)
    nf = pl.num_programs(1)

    @pl.when(f == 0)
    def _():
        h_ref[...] = _rms(x_ref[...], gf_ref[...]).astype(BF)
        acc_ref[...] = jnp.zeros_like(acc_ref)

    @pl.when(i % nblk_seq == 0)
    def _():
        cg_ref[f] = jnp.zeros((CONV_HALO, cg_ref.shape[2]), F32)
        cv_ref[f] = jnp.zeros((CONV_HALO, cv_ref.shape[2]), F32)

    tf = wug_ref.shape[1]
    rc = tb // row_chunks
    tmod = lax.broadcasted_iota(jnp.int32, (rc, tf), 0) % seg
    head_row = lax.broadcasted_iota(jnp.int32, (CONV_HALO, tf), 0)

    def shifted(a, carry, k):
        moved = pltpu.roll(a, k, 0)
        first = jnp.where(head_row < k, pltpu.roll(carry, k, 0), moved[0:CONV_HALO])
        return jnp.concatenate([first, moved[CONV_HALO:]], axis=0) if rc > CONV_HALO else first

    def conv(a, carry, cw_ref, cb_ref, state, rows):
        prev1 = shifted(a, carry, 1)
        prev2 = shifted(a, carry, 2)
        if with_state:
            prev1 = jnp.where(tmod < 1, state[0][rows, :], prev1)
            prev2 = jnp.where(tmod < 2, state[1][rows, :], prev2)
        cw = cw_ref[...]
        return cb_ref[...] + prev2 * cw[0:1, :] + prev1 * cw[1:2, :] + a * cw[2:3, :]

    carry_g, carry_v = cg_ref[f], cv_ref[f]
    for c in range(row_chunks):
        rows = slice(c * rc, (c + 1) * rc)
        hc = h_ref[rows, :]
        ag = _dot(hc, wug_ref[...])
        av = _dot(hc, wuv_ref[...])
        cgate = conv(ag, carry_g, cwg_ref, cbg_ref, (p1g_ref, p2g_ref) if with_state else None, rows)
        cval = conv(av, carry_v, cwv_ref, cbv_ref, (p1v_ref, p2v_ref) if with_state else None, rows)
        carry_g, carry_v = ag[rc - CONV_HALO:rc, :], av[rc - CONV_HALO:rc, :]
        if with_state:
            sg_ref[rows, :] = ag
            sv_ref[rows, :] = av
        acc_ref[rows, :] += _dot((_gelu_tanh(cgate) * cval).astype(BF), wd_ref[...])
    cg_ref[f] = carry_g
    cv_ref[f] = carry_v
    if not with_state:
        sg_ref[...] = carry_g[CONV_HALO - (CONV_W - 1):CONV_HALO, :]
        sv_ref[...] = carry_v[CONV_HALO - (CONV_W - 1):CONV_HALO, :]

    @pl.when(f == nf - 1)
    def _():
        x2 = x_ref[...] + acc_ref[...]
        gate = jax.nn.sigmoid(_dot(_rms(x2, gp_ref[...]).astype(BF), wpg_ref[...]))
        x3 = x2 + gate * _dot(pe_ref[...].astype(BF), wpp_ref[...])
        if final:
            x3 = _rms(x3, gfin_ref[...])
        o_ref[...] = x3


def _ffn(x, pe, g_ffn, w_up, conv_w, conv_b, w_down, g_ple, w_pg, w_pp, g_final, state, *, tb, seg, tf, final):
    n, d = x.shape
    dff = w_down.shape[0]
    nf = dff // tf
    with_state = state is not None
    nblk = n // tb
    nblk_seq = max(1, seg // tb)
    row_chunks = 1
    kern = functools.partial(_ffn_kernel, tb=tb, seg=seg, nblk_seq=nblk_seq, final=final, with_state=with_state,
                             row_chunks=row_chunks)
    tok = lambda w: pl.BlockSpec((tb, w), lambda i, f: (i, 0))
    cst = lambda shape: pl.BlockSpec(shape, lambda i, f: (0,) * len(shape))
    in_specs = [tok(d), tok(pe.shape[1]), cst((1, d)),
                pl.BlockSpec((d, tf), lambda i, f: (0, f)), pl.BlockSpec((d, tf), lambda i, f: (0, nf + f)),
                pl.BlockSpec((CONV_W, tf), lambda i, f: (0, f)), pl.BlockSpec((CONV_W, tf), lambda i, f: (0, nf + f)),
                pl.BlockSpec((1, tf), lambda i, f: (0, f)), pl.BlockSpec((1, tf), lambda i, f: (0, nf + f)),
                pl.BlockSpec((tf, d), lambda i, f: (f, 0)),
                cst((1, d)), _resident(w_pg.shape), _resident(w_pp.shape), cst((1, d))]
    args = [x, pe, g_ffn, w_up, w_up, conv_w, conv_w, conv_b, conv_b, w_down, g_ple, w_pg, w_pp, g_final]
    if with_state:
        p1, p2 = state
        in_specs += [pl.BlockSpec((tb, tf), lambda i, f: (i, f)), pl.BlockSpec((tb, tf), lambda i, f: (i, nf + f)),
                     pl.BlockSpec((tb, tf), lambda i, f: (i, f)), pl.BlockSpec((tb, tf), lambda i, f: (i, nf + f))]
        args += [p1, p1, p2, p2]
        st_shape = jax.ShapeDtypeStruct((n, dff), F32)
        st_spec = pl.BlockSpec((tb, tf), lambda i, f: (i, f))
    else:
        st_shape = jax.ShapeDtypeStruct((nblk, CONV_W - 1, dff), F32)
        st_spec = pl.BlockSpec((None, CONV_W - 1, tf), lambda i, f: (i, 0, f))
    return pl.pallas_call(
        kern,
        grid=(nblk, nf),
        in_specs=in_specs,
        out_specs=[tok(d), st_spec, st_spec],
        out_shape=[jax.ShapeDtypeStruct((n, d), F32), st_shape, st_shape],
        scratch_shapes=[pltpu.VMEM((tb, d), BF), pltpu.VMEM((tb, d), F32),
                        pltpu.VMEM((nf, CONV_HALO, tf), F32), pltpu.VMEM((nf, CONV_HALO, tf), F32)],
        compiler_params=_cparams(("arbitrary", "arbitrary")),
        name="ffn",
    )(*args)


def _pick_block(n, target):
    b = min(n, target)
    while n % b:
        b //= 2
    return b


def kernel(x_prompt, x_sample, cache_fox_k, cache_fox_v, cache_fox_logf, cache_sb_k, cache_sb_v, state_pool, state_ffn_conv, page_table, p_prompt, p_sample, g_mix, w_in, b_f, gm_ln_g, gm_ln_b, gm_ws, gm_bs, pm_w, pm_scale, w_br, w_out, g_ffn, w_up, conv_w, conv_b, w_down, g_ple, w_ple_gate, w_ple_proj, g_final):
    nbp, t, d = x_prompt.shape
    nbs, ts, _ = x_sample.shape
    depth = w_in.shape[0]
    n_pool, page = cache_fox_k.shape[1], cache_fox_k.shape[2]
    n_pages = page_table.shape[1]
    past = n_pages * page
    dff = w_down.shape[1]
    f2 = 2 * dff
    off_mix = N_BRANCH * d
    off_f = off_mix + 6 * BR_WIDTH
    off_sbq = off_f + N_HEADS
    np_tok, ns_tok = nbp * t, nbs * ts

    tb_p = _pick_block(t, 512)
    tb_ffn = _pick_block(t, 512)
    tq = _pick_block(t, 256)
    tf = dff // 2 if (dff // 2) % LANES == 0 else 256
    npp = _pick_block(n_pages, 8)

    row = lambda a: a.reshape(1, -1)
    xp = x_prompt.reshape(np_tok, d)
    xs = x_sample.reshape(ns_tok, d)
    kv_pages = lambda c: jnp.transpose(c, (0, 1, 3, 4, 2)).reshape(depth, n_pool, BR_WIDTH, page)
    ck_f, cv_f, ck_s, cv_s = kv_pages(cache_fox_k), kv_pages(cache_fox_v), kv_pages(cache_sb_k), kv_pages(cache_sb_v)
    clf = jnp.pad(jnp.swapaxes(cache_fox_logf, 2, 3), ((0, 0), (0, 0), (0, SUBLANES - N_HEADS), (0, 0)))

    gm_tile = lambda a: jnp.tile(a[:, :ts, :ts], (1, nbs, nbs))

    w_t = jnp.transpose(w_in, (2, 0, 1))
    blocks = lambda start, n: [start + W_BLOCK * j for j in range(n // W_BLOCK)]
    mix_starts = (blocks(off_mix, 3 * BR_WIDTH) + blocks(off_mix + 3 * BR_WIDTH, BR_WIDTH) + blocks(off_sbq, BR_WIDTH)
                  + [off_f]
                  + blocks(off_mix + 4 * BR_WIDTH, 2 * BR_WIDTH) + blocks(off_sbq + BR_WIDTH, 2 * BR_WIDTH))
    w_gate_all = _wprep(w_t, blocks(0, off_mix))
    w_mix_all = _wprep(w_t, mix_starts)

    new_p = [[] for _ in range(7)]
    new_s = [[] for _ in range(8)]
    for i in range(depth):
        bf_row = jnp.pad(b_f[i], (0, LANES - N_HEADS)).reshape(1, LANES)
        w_br_b = w_br[i].astype(BF)
        w_out_b = w_out[i].astype(BF)
        w_up_b = w_up[i].astype(BF)
        w_down_b = w_down[i].astype(BF)
        w_pg = w_ple_gate[i].astype(BF)
        w_pp = w_ple_proj[i].astype(BF)
        w_pm_bd = jax.scipy.linalg.block_diag(*[pm_w[i, g] for g in range(len(POOL_WINDOWS))]).astype(BF)
        bs_full = lambda rows: jnp.repeat(jnp.tile(gm_bs[i][:, :min(rows, GM_CHUNK)].T, (rows // min(rows, GM_CHUNK), 1)),
                                          HEAD_DIM, axis=1)
        common = (row(g_mix[i]), w_mix_all, bf_row, row(gm_ln_g[i]), row(gm_ln_b[i]))
        final = i == depth - 1
        heads = lambda a: jnp.transpose(a.reshape(a.shape[0], N_HEADS, HEAD_DIM, a.shape[2]), (0, 3, 1, 2))
        logf = lambda a: jnp.transpose(a[:, :N_HEADS], (0, 2, 1))

        gm_rows = min(GM_CHUNK, tb_p)
        (o_gm, pool_in, fq, sq, _, cum_c, lft, cum_r, fk, fv, sk, sv, fkb, fvb, skb, svb) = _inproj(
            xp, *common, gm_ws[i][:, :gm_rows, :gm_rows], bs_full(gm_rows),
            layer=i, tb=tb_p, seg=t, gm_rows=gm_rows, gm_chunk=GM_CHUNK, nseq=nbp)
        o_pool = _pool(pool_in, w_pm_bd, row(pm_scale[i]), tb=tb_p, rows_per_seq=t, offset=0, pos0=0)
        o_fox = _fox(fq, fkb, fvb, cum_c, cum_r, nseq=nbp, t=t, tq=tq)
        o_sb = _sb(sq, skb, svb, nseq=nbp, t=t, tq=tq)
        x1 = _merge(xp, o_gm, o_pool, o_fox, o_sb, row(g_mix[i]), w_gate_all, w_br_b, w_out_b, layer=i, tb=tb_p)
        xp, st_g, st_v = _ffn(x1, p_prompt[i].reshape(np_tok, -1), row(g_ffn[i]), w_up_b, conv_w[i], row(conv_b[i]),
                              w_down_b, row(g_ple[i]), w_pg, w_pp, row(g_final), None,
                              tb=tb_ffn, seg=t, tf=tf, final=final)
        new_p[0].append(heads(fk))
        new_p[1].append(heads(fv))
        new_p[2].append(logf(lft))
        new_p[3].append(heads(sk))
        new_p[4].append(heads(sv))
        new_p[5].append(pool_in.reshape(nbp, t, BR_WIDTH)[:, t - POOL_BUF:])
        last = slice(t // tb_ffn - 1, None, t // tb_ffn)
        new_p[6].append(jnp.concatenate([st_g[last], st_v[last]], axis=-1))

        (o_gm, pool_in, fq, sq, vn, _, lft, _, fk, fv, sk, sv, _, _, _, _) = _inproj(
            xs, *common, gm_tile(gm_ws[i]), bs_full_sample(gm_bs[i], ts, nbs),
            layer=i, tb=ns_tok, seg=ts, gm_rows=ns_tok, gm_chunk=ts, nseq=1)
        grp_rows = -(-(POOL_BUF + ts) // SUBLANES) * SUBLANES
        lead = grp_rows - POOL_BUF - ts
        full = jnp.concatenate([jnp.zeros((nbs, lead, BR_WIDTH), F32), state_pool[i],
                                pool_in.reshape(nbs, ts, BR_WIDTH)], axis=1)
        o_pool = _pool(full.reshape(nbs * grp_rows, BR_WIDTH), w_pm_bd, row(pm_scale[i]),
                       tb=nbs * grp_rows, rows_per_seq=grp_rows, offset=grp_rows - ts, pos0=past)
        o_pool = o_pool.reshape(nbs, grp_rows, BR_WIDTH)[:, grp_rows - ts:].reshape(ns_tok, BR_WIDTH)
        o_fox, o_sb = _dec_attn(
            page_table, _block_diag_queries(fq, nbs, ts), _block_diag_queries(sq, nbs, ts),
            _pad_new(fk, nbs, ts), _pad_new(fv, nbs, ts),
            jnp.pad(jnp.swapaxes(lft.reshape(SUBLANES, nbs, ts), 0, 1), ((0, 0), (0, 0), (0, LANES - ts))),
            _pad_new(sk, nbs, ts), _pad_new(sv, nbs, ts),
            ck_f, cv_f, clf, ck_s, cv_s, layer=i, npp=npp, tnew=ts)
        o_fox = o_fox.reshape(ns_tok, BR_WIDTH).astype(BF)
        o_sb = o_sb.reshape(ns_tok, BR_WIDTH).astype(BF)
        x1 = _merge(xs, o_gm, o_pool, o_fox, o_sb, row(g_mix[i]), w_gate_all, w_br_b, w_out_b, layer=i, tb=ns_tok)
        buf = state_ffn_conv[i]
        zero = jnp.zeros((nbs, ts - 1, f2), F32)
        p1 = jnp.concatenate([buf[:, 1:2], zero], axis=1).reshape(ns_tok, f2)
        p2 = jnp.concatenate([buf, zero[:, 1:]], axis=1).reshape(ns_tok, f2)
        xs, a_g, a_v = _ffn(x1, p_sample[i].reshape(ns_tok, -1), row(g_ffn[i]), w_up_b, conv_w[i], row(conv_b[i]),
                            w_down_b, row(g_ple[i]), w_pg, w_pp, row(g_final), (p1, p2),
                            tb=ns_tok, seg=ts, tf=tf, final=final)
        new_s[0].append(heads(fk).reshape(nbs, ts, N_HEADS, HEAD_DIM))
        new_s[1].append(heads(fv).reshape(nbs, ts, N_HEADS, HEAD_DIM))
        new_s[2].append(logf(lft).reshape(nbs, ts, N_HEADS))
        new_s[3].append(heads(sk).reshape(nbs, ts, N_HEADS, HEAD_DIM))
        new_s[4].append(heads(sv).reshape(nbs, ts, N_HEADS, HEAD_DIM))
        new_s[5].append(jnp.concatenate([state_pool[i], pool_in.reshape(nbs, ts, BR_WIDTH)], axis=1)[:, ts:])
        a_full = jnp.concatenate([a_g, a_v], axis=-1).reshape(nbs, ts, f2)
        new_s[6].append(jnp.concatenate([buf, a_full], axis=1)[:, ts:])
        new_s[7].append(vn.reshape(nbs, ts, BR_WIDTH))

    stk = lambda lst: jnp.stack(lst, axis=0)
    return (xp.reshape(nbp, t, d), xs.reshape(nbs, ts, d),
            *[stk(a) for a in new_p], *[stk(a) for a in new_s])


def bs_full_sample(gm_bs_i, ts, nbs):
    return jnp.repeat(jnp.tile(gm_bs_i[:, :ts].T, (nbs, 1)), HEAD_DIM, axis=1)


def _block_diag_queries(q, nbs, ts):
    q = q.reshape(nbs, ts, 1, N_HEADS, HEAD_DIM)
    eye = jnp.eye(SUBLANES, N_HEADS, dtype=q.dtype).reshape(1, 1, SUBLANES, N_HEADS, 1)
    return (q * eye).reshape(nbs, ts * SUBLANES, BR_WIDTH)


def _pad_new(a, nbs, ts):
    return jnp.pad(jnp.swapaxes(a.reshape(BR_WIDTH, nbs, ts), 0, 1), ((0, 0), (0, 0), (0, LANES - ts)))
```

```python
import functools

import jax
import jax.numpy as jnp
from jax import lax
from jax.experimental import pallas as pl
from jax.experimental.pallas import tpu as pltpu

F32 = jnp.float32
BF = jnp.bfloat16

EPS = 1e-6
N_BRANCH = 4
BR_WIDTH = 256
N_HEADS = 4
HEAD_DIM = 64
LANES = 128
SUBLANES = 8
POOL_WINDOWS = (2, 4, 8, 16)
POOL_BUF = 15
POOL_HALO = 16
CONV_W = 3
GM_CHUNK = 128
NEG_BIG = -1e30
EXP_ZERO_BELOW = -104.0
VMEM_LIMIT = 48 * 1024 * 1024

C_U, C_V, C_POOL, C_FQ, C_SQ, C_F = [BR_WIDTH * i for i in range(6)]
N_TOKMAJ = C_F + LANES
R_FK, R_FV, R_SK, R_SV = [BR_WIDTH * i for i in range(4)]
N_HEADMAJ = 4 * BR_WIDTH
W_BLOCK = 128


def _cparams(sem):
    return pltpu.CompilerParams(dimension_semantics=sem, vmem_limit_bytes=VMEM_LIMIT)


def _rms(x, g):
    return x * lax.rsqrt(jnp.mean(x * x, axis=-1, keepdims=True) + EPS) * g


def _softplus(x):
    return jnp.maximum(x, 0.0) + jnp.log1p(jnp.exp(-jnp.abs(x)))


def _split3(x):
    hi = x.astype(BF)
    r = x - hi.astype(F32)
    mid = r.astype(BF)
    r = r - mid.astype(F32)
    return hi, mid, r.astype(BF)


def _dot(a, b):
    return jnp.dot(a, b, preferred_element_type=F32)


def _dot_nt(a, b):
    return lax.dot_general(a, b, (((1,), (1,)), ((), ())), preferred_element_type=F32)


def _dot_exact(x, ones_bf, left, terms=3):
    parts = _split3(x)[:terms]
    out = None
    for p in parts:
        d = _dot(ones_bf, p) if left else _dot(p, ones_bf)
        out = d if out is None else out + d
    return out


def _const_spec(shape):
    nd = len(shape)
    return pl.BlockSpec(shape, lambda *_: (0,) * nd)


def _wprep_kernel(st_ref, w_ref, o_ref):
    del st_ref
    for layer in range(o_ref.shape[0]):
        o_ref[layer] = w_ref[:, layer, :].astype(BF)


def _wprep(w_t, starts):
    n_out, depth, d = w_t.shape
    nblk = len(starts)
    grid_spec = pltpu.PrefetchScalarGridSpec(
        num_scalar_prefetch=1,
        grid=(nblk,),
        in_specs=[pl.BlockSpec((pl.Element(W_BLOCK), pl.Element(depth), pl.Element(d)), lambda j, st: (st[j], 0, 0))],
        out_specs=pl.BlockSpec((depth, W_BLOCK, d), lambda j, st: (0, j, 0)),
    )
    return pl.pallas_call(
        _wprep_kernel,
        grid_spec=grid_spec,
        out_shape=jax.ShapeDtypeStruct((depth, nblk * W_BLOCK, d), BF),
        compiler_params=_cparams(("arbitrary",)),
        name="wprep",
    )(jnp.asarray(starts, jnp.int32), w_t)


def _inproj_kernel(x_ref, g_ref, w_ref, bfr_ref, lng_ref, lnb_ref, ws_ref, bs_ref,
                   ogm_ref, pool_ref, fq_ref, sq_ref, vn_ref, cc_ref, lft_ref, cr_ref,
                   fk_ref, fv_ref, sk_ref, sv_ref, fkb_ref, fvb_ref, skb_ref, svb_ref,
                   carc_ref, *, tb, seg, gm_rows, gm_chunk):
    i = pl.program_id(0)
    h = _rms(x_ref[...], g_ref[...]).astype(BF)
    z = _dot_nt(h, w_ref[0:N_TOKMAJ, :])
    zt = _dot_nt(w_ref[N_TOKMAJ:N_TOKMAJ + N_HEADMAJ, :], h)

    r = lax.broadcasted_iota(jnp.int32, (gm_rows, gm_rows), 0)
    c = lax.broadcasted_iota(jnp.int32, (gm_rows, gm_rows), 1)
    mix_mask = (r >= c) & ((r // gm_chunk) == (c // gm_chunk))
    grp = lax.broadcasted_iota(jnp.int32, (gm_rows, BR_WIDTH), 1) // HEAD_DIM
    for blk in range(tb // gm_rows):
        rows = slice(blk * gm_rows, (blk + 1) * gm_rows)
        u = z[rows, C_U:C_U + BR_WIDTH]
        v = z[rows, C_V:C_V + BR_WIDTH]
        mu = jnp.mean(v, axis=-1, keepdims=True)
        var = jnp.mean(jnp.square(v - mu), axis=-1, keepdims=True)
        vn = (v - mu) * lax.rsqrt(var + EPS) * lng_ref[...] + lnb_ref[...]
        vn_ref[rows, :] = vn
        vnb = vn.astype(BF)
        s = bs_ref[...]
        for g in range(N_HEADS):
            wg = jnp.where(mix_mask, ws_ref[g], 0.0).astype(BF)
            s = s + jnp.where(grp == g, _dot(wg, vnb), 0.0)
        ogm_ref[rows, :] = (u * s).astype(BF)

    pool_ref[...] = z[:, C_POOL:C_POOL + BR_WIDTH]
    scale = HEAD_DIM ** -0.5
    fq_ref[...] = (z[:, C_FQ:C_FQ + BR_WIDTH] * scale).astype(BF)
    sq_ref[...] = (z[:, C_SQ:C_SQ + BR_WIDTH] * scale).astype(BF)
    for start, full_ref, half_ref in ((R_FK, fk_ref, fkb_ref), (R_FV, fv_ref, fvb_ref),
                                      (R_SK, sk_ref, skb_ref), (R_SV, sv_ref, svb_ref)):
        part = zt[start:start + BR_WIDTH, :]
        full_ref[...] = part
        half_ref[...] = part.astype(BF)

    lf = -_softplus(-(z[:, C_F:C_F + LANES] + bfr_ref[...]))
    lft_ref[...] = lf.T[0:SUBLANES, :]
    r = lax.broadcasted_iota(jnp.int32, (tb, tb), 0)
    c = lax.broadcasted_iota(jnp.int32, (tb, tb), 1)
    if seg >= tb:
        lower = (r >= c)
        nblk_seq = seg // tb

        @pl.when(i % nblk_seq == 0)
        def _():
            carc_ref[...] = jnp.zeros_like(carc_ref)
    else:
        lower = (r >= c) & ((r // seg) == (c // seg))
        carc_ref[...] = jnp.zeros_like(carc_ref)
    cum_c = _dot_exact(lf, jnp.where(lower, 1.0, 0.0).astype(BF), left=True) + carc_ref[0:1, :]
    cc_ref[...] = cum_c
    cr_ref[...] = cum_c.T[0:SUBLANES, :]
    carc_ref[...] = jnp.broadcast_to(cum_c[tb - 1:tb, :], carc_ref.shape)


def _inproj(x, g_mix, w_mix, bf_row, ln_g, ln_b, ws, bs, *, layer, tb, seg, gm_rows, gm_chunk, nseq):
    n, d = x.shape
    kern = functools.partial(_inproj_kernel, tb=tb, seg=seg, gm_rows=gm_rows, gm_chunk=gm_chunk)
    tok = lambda w: pl.BlockSpec((tb, w), lambda i: (i, 0))
    per_seq = n // nseq
    nblk_seq = per_seq // tb
    chan = lambda rows: pl.BlockSpec((None, rows, tb), lambda i: (i // nblk_seq, 0, i % nblk_seq))
    tokmaj = lambda width, dt: jax.ShapeDtypeStruct((n, width), dt)
    chanmaj = lambda rows, dt: jax.ShapeDtypeStruct((nseq, rows, per_seq), dt)
    out_shape = [
        tokmaj(BR_WIDTH, BF),
        tokmaj(BR_WIDTH, F32),
        tokmaj(BR_WIDTH, BF),
        tokmaj(BR_WIDTH, BF),
        tokmaj(BR_WIDTH, F32),
        tokmaj(LANES, F32),
        chanmaj(SUBLANES, F32),
        chanmaj(SUBLANES, F32),
    ] + [chanmaj(BR_WIDTH, F32)] * 4 + [chanmaj(BR_WIDTH, BF)] * 4
    out_specs = ([tok(BR_WIDTH)] * 5 + [tok(LANES), chan(SUBLANES), chan(SUBLANES)] + [chan(BR_WIDTH)] * 8)
    w_rows = w_mix.shape[1]
    return pl.pallas_call(
        kern,
        grid=(n // tb,),
        in_specs=[tok(d), _const_spec((1, d)),
                  pl.BlockSpec((None, w_rows, d), lambda i: (layer, 0, 0), pipeline_mode=pl.Buffered(1)),
                  _const_spec(bf_row.shape), _const_spec(ln_g.shape),
                  _const_spec(ln_b.shape), _const_spec(ws.shape), _const_spec(bs.shape)],
        out_specs=out_specs,
        out_shape=out_shape,
        scratch_shapes=[pltpu.VMEM((SUBLANES, LANES), F32)],
        compiler_params=_cparams(("arbitrary",)),
        name="inproj",
    )(x, g_mix, w_mix, bf_row, ln_g, ln_b, ws, bs)


def _pool_kernel(x_ref, w_ref, sc_ref, o_ref, ext_ref, *, tb, rows_per_seq, offset, pos0, nblk_seq):
    i = pl.program_id(0)

    @pl.when(i % nblk_seq == 0)
    def _():
        ext_ref[0:POOL_HALO, :] = jnp.zeros((POOL_HALO, BR_WIDTH), F32)

    ext_ref[POOL_HALO:POOL_HALO + tb, :] = x_ref[...]
    rowi = lax.broadcasted_iota(jnp.int32, (tb, LANES), 0)
    lane = lax.broadcasted_iota(jnp.int32, (tb, LANES), 1)
    pos = pos0 + (i * tb + rowi) % rows_per_seq - offset
    first = lane < HEAD_DIM
    halves = []
    for half in range(2):
        cols = slice(half * LANES, (half + 1) * LANES)
        shifted = lambda j: ext_ref[POOL_HALO - j:POOL_HALO - j + tb, cols]
        w_small, w_big = POOL_WINDOWS[2 * half], POOL_WINDOWS[2 * half + 1]
        cur = shifted(0)
        acc = cur
        for j in range(1, w_small):
            acc = acc + shifted(j)
        small = acc
        for j in range(w_small, w_big):
            acc = acc + shifted(j)
        win = jnp.where(first, small, acc)
        width = jnp.where(first, w_small, w_big)
        cnt = jnp.clip(pos + 1, 1, width).astype(F32)
        halves.append(win / cnt - cur)
    d = jnp.concatenate(halves, axis=1).astype(BF)
    o_ref[...] = (_dot(d, w_ref[...]) * sc_ref[...]).astype(BF)
    ext_ref[0:POOL_HALO, :] = ext_ref[tb:tb + POOL_HALO, :]


def _pool(x, w_bd, scale, *, tb, rows_per_seq, offset, pos0):
    n = x.shape[0]
    kern = functools.partial(_pool_kernel, tb=tb, rows_per_seq=rows_per_seq, offset=offset, pos0=pos0,
                             nblk_seq=max(1, rows_per_seq // tb))
    return pl.pallas_call(
        kern,
        grid=(n // tb,),
        in_specs=[pl.BlockSpec((tb, BR_WIDTH), lambda i: (i, 0)), _const_spec(w_bd.shape), _const_spec(scale.shape)],
        out_specs=pl.BlockSpec((tb, BR_WIDTH), lambda i: (i, 0)),
        out_shape=jax.ShapeDtypeStruct((n, BR_WIDTH), BF),
        scratch_shapes=[pltpu.VMEM((tb + POOL_HALO, BR_WIDTH), F32)],
        compiler_params=_cparams(("arbitrary",)),
        name="pool",
    )(x, w_bd, scale)


def _head_queries(q):
    left = lax.broadcasted_iota(jnp.int32, (q.shape[0], LANES), 1) < HEAD_DIM
    out = []
    for h in range(N_HEADS):
        pair = q[:, (h // 2) * LANES:(h // 2 + 1) * LANES]
        keep = left if h % 2 == 0 else jnp.logical_not(left)
        out.append(jnp.where(keep, pair, jnp.zeros_like(pair)))
    return out, left


def _fox_kernel(q_ref, k_ref, v_ref, cc_ref, cr_ref, o_ref, kn_ref, *, tq):
    qb = pl.program_id(1)
    qh, left = _head_queries(q_ref[...])
    cc = cc_ref[...]
    cq = [cc[:, h:h + 1] for h in range(N_HEADS)]
    r = lax.broadcasted_iota(jnp.int32, (tq, tq), 0)
    c = lax.broadcasted_iota(jnp.int32, (tq, tq), 1)
    causal = c <= r

    @pl.when(qb == 0)
    def _():
        hr = lax.broadcasted_iota(jnp.int32, (SUBLANES, BR_WIDTH), 0)
        hc = lax.broadcasted_iota(jnp.int32, (SUBLANES, BR_WIDTH), 1) // HEAD_DIM
        head_rows = jnp.where(hr == hc, 1.0, 0.0).astype(BF)

        def chunk(ci, best):
            kc = k_ref[:, pl.ds(pl.multiple_of(ci * tq, tq), tq)].astype(F32)
            n2 = _dot_exact(kc * kc, head_rows, left=True)
            return jnp.maximum(best, jnp.max(n2, axis=1, keepdims=True))

        best = lax.fori_loop(0, k_ref.shape[1] // tq, chunk, jnp.zeros((SUBLANES, 1), F32))
        kn_ref[...] = jnp.broadcast_to(best, kn_ref.shape)

    reach = []
    for h in range(N_HEADS):
        qf = qh[h].astype(F32)
        qn2 = jnp.sum(qf * qf, axis=-1, keepdims=True)
        reach.append(jnp.sqrt(qn2 * kn_ref[h:h + 1, 0:1]) * 1.001 + 0.01 + cq[h])

    def block(j, carry, masked):
        ms, ls, accs = carry
        start = pl.multiple_of(j * tq, tq)
        kblk = k_ref[:, pl.ds(start, tq)]
        vblk = v_ref[:, pl.ds(start, tq)]
        heads = range(N_HEADS)
        pairs = [slice((h // 2) * LANES, (h // 2 + 1) * LANES) for h in heads]
        ss = [_dot(qh[h], kblk[pairs[h], :]) + (cq[h] - cr_ref[h:h + 1, pl.ds(start, tq)]) for h in heads]
        if masked:
            ss = [jnp.where(causal, s, NEG_BIG) for s in ss]
        new_m = [jnp.maximum(ms[h], jnp.max(ss[h], axis=-1, keepdims=True)) for h in heads]
        alphas = [jnp.exp(ms[h] - new_m[h]) for h in heads]
        ps = [jnp.exp(ss[h] - new_m[h]) for h in heads]
        new_l = [ls[h] * alphas[h] + jnp.sum(ps[h], axis=-1, keepdims=True) for h in heads]
        pv = [_dot_nt(ps[h].astype(BF), vblk[pairs[h], :]) for h in heads]
        new_acc = []
        for hp in range(N_HEADS // 2):
            a = jnp.where(left, alphas[2 * hp], alphas[2 * hp + 1])
            new_acc.append(accs[hp] * a + jnp.where(left, pv[2 * hp], pv[2 * hp + 1]))
        return tuple(new_m), tuple(new_l), tuple(new_acc)

    init = (tuple(jnp.full((tq, 1), NEG_BIG, F32) for _ in range(N_HEADS)),
            tuple(jnp.zeros((tq, 1), F32) for _ in range(N_HEADS)),
            tuple(jnp.zeros((tq, LANES), F32) for _ in range(N_HEADS // 2)))
    def live(state):
        j, ms = state[0], state[1]
        start = pl.multiple_of(jnp.maximum(j, 0) * tq, tq)
        gap = None
        for h in range(N_HEADS):
            ck_last = cr_ref[h:h + 1, pl.ds(start, tq)][:, tq - 1:tq]
            g = reach[h] - ck_last - ms[h]
            gap = g if gap is None else jnp.maximum(gap, g)
        return (j >= 0) & (jnp.max(gap) > EXP_ZERO_BELOW)

    def step(state):
        j = state[0]
        return (j - 1,) + block(j, state[1:], False)

    first = block(qb, init, True)
    _, ms, ls, accs = lax.while_loop(live, step, (qb - 1,) + first)
    for hp in range(N_HEADS // 2):
        den = jnp.where(left, ls[2 * hp], ls[2 * hp + 1])
        o_ref[:, hp * LANES:(hp + 1) * LANES] = (accs[hp] / den).astype(BF)


def _fox(q, kb, vb, cum_c, cum_r, *, nseq, t, tq):
    kern = functools.partial(_fox_kernel, tq=tq)
    nq = t // tq
    return pl.pallas_call(
        kern,
        grid=(nseq, nq),
        in_specs=[pl.BlockSpec((tq, BR_WIDTH), lambda n, i: (n * nq + i, 0)),
                  pl.BlockSpec((None, BR_WIDTH, t), lambda n, i: (n, 0, 0)),
                  pl.BlockSpec((None, BR_WIDTH, t), lambda n, i: (n, 0, 0)),
                  pl.BlockSpec((tq, LANES), lambda n, i: (n * nq + i, 0)),
                  pl.BlockSpec((None, SUBLANES, t), lambda n, i: (n, 0, 0))],
        out_specs=pl.BlockSpec((tq, BR_WIDTH), lambda n, i: (n * nq + i, 0)),
        out_shape=jax.ShapeDtypeStruct((nseq * t, BR_WIDTH), BF),
        scratch_shapes=[pltpu.VMEM((SUBLANES, LANES), F32)],
        compiler_params=_cparams(("arbitrary", "arbitrary")),
        name="fox_prompt",
    )(q, kb, vb, cum_c, cum_r)


def _sb_kernel(q_ref, k_ref, v_ref, o_ref, *, tq):
    qb = pl.program_id(1)
    qh, left = _head_queries(q_ref[...])
    r = lax.broadcasted_iota(jnp.int32, (tq, tq), 0)
    c = lax.broadcasted_iota(jnp.int32, (tq, tq), 1)
    strict = c < r
    after = jnp.where(r > c, 1.0, 0.0).astype(BF)

    def block(j, carry, masked):
        cs, accs = carry
        start = pl.multiple_of(j * tq, tq)
        kblk = k_ref[:, pl.ds(start, tq)]
        vblk = v_ref[:, pl.ds(start, tq)]
        pairs = [slice((h // 2) * LANES, (h // 2 + 1) * LANES) for h in range(N_HEADS)]
        zs = [_dot(qh[h], kblk[pairs[h], :]) for h in range(N_HEADS)]
        lrests = [-_softplus(z) for z in zs]
        if masked:
            lrests = [jnp.where(strict, l, 0.0) for l in lrests]
        laters = [_dot_exact(lrests[h], after, left=False, terms=2) + cs[h] for h in range(N_HEADS)]
        weights = [jnp.exp(zs[h] + lrests[h] + laters[h]) for h in range(N_HEADS)]
        if masked:
            weights = [jnp.where(strict, a, 0.0) for a in weights]
        av = [_dot_nt(weights[h].astype(BF), vblk[pairs[h], :]) for h in range(N_HEADS)]
        new_c = [cs[h] + jnp.sum(lrests[h], axis=-1, keepdims=True) for h in range(N_HEADS)]
        new_acc = tuple(accs[hp] + jnp.where(left, av[2 * hp], av[2 * hp + 1]) for hp in range(N_HEADS // 2))
        return tuple(new_c), new_acc

    init = (tuple(jnp.zeros((tq, 1), F32) for _ in range(N_HEADS)),
            tuple(jnp.zeros((tq, LANES), F32) for _ in range(N_HEADS // 2)))
    def live(state):
        j, cs = state[0], state[1]
        top = cs[0]
        for h in range(1, N_HEADS):
            top = jnp.maximum(top, cs[h])
        return (j >= 0) & (jnp.max(top) > EXP_ZERO_BELOW)

    def step(state):
        j = state[0]
        return (j - 1,) + block(j, state[1:], False)

    first = block(qb, init, True)
    _, _, accs = lax.while_loop(live, step, (qb - 1,) + first)
    for hp in range(N_HEADS // 2):
        o_ref[:, hp * LANES:(hp + 1) * LANES] = accs[hp].astype(BF)


def _sb(q, kb, vb, *, nseq, t, tq):
    kern = functools.partial(_sb_kernel, tq=tq)
    nq = t // tq
    return pl.pallas_call(
        kern,
        grid=(nseq, nq),
        in_specs=[pl.BlockSpec((tq, BR_WIDTH), lambda n, i: (n * nq + i, 0)),
                  pl.BlockSpec((None, BR_WIDTH, t), lambda n, i: (n, 0, 0)),
                  pl.BlockSpec((None, BR_WIDTH, t), lambda n, i: (n, 0, 0))],
        out_specs=pl.BlockSpec((tq, BR_WIDTH), lambda n, i: (n * nq + i, 0)),
        out_shape=jax.ShapeDtypeStruct((nseq * t, BR_WIDTH), BF),
        compiler_params=_cparams(("arbitrary", "arbitrary")),
        name="sb_prompt",
    )(q, kb, vb)


N_CACHES = 5
KF, VF, LF, KS, VS = range(N_CACHES)


def _dec_attn_kernel(pt_ref, qf_ref, qs_ref, knf_ref, vnf_ref, lfn_ref, kns_ref, vns_ref,
                     ckf_hbm, cvf_hbm, clf_hbm, cks_hbm, cvs_hbm, of_ref, os_ref,
                     kf_buf, vf_buf, lf_buf, ks_buf, vs_buf, sems,
                     m_ref, l_ref, accf_ref, offf_ref, e_ref, cs_ref, accs_ref,
                     *, layer, npp, n_pages, tnew):
    DEC_ROWS = tnew * SUBLANES
    b_id = pl.program_id(0)
    n_chunks = n_pages // npp
    hbm = (ckf_hbm, cvf_hbm, clf_hbm, cks_hbm, cvs_hbm)
    bufs = (kf_buf, vf_buf, lf_buf, ks_buf, vs_buf)

    def chunk_copies(which, chunk, slot):
        first = n_pages - (chunk + 1) * npp
        return [pltpu.make_async_copy(hbm[which].at[layer, pt_ref[b_id, first + j]],
                                      bufs[which].at[slot, j], sems.at[which, slot]) for j in range(npp)]

    def start(which, chunk, slot):
        for cp in chunk_copies(which, chunk, slot):
            cp.start()

    def wait(which, chunk, slot):
        for cp in chunk_copies(which, chunk, slot):
            cp.wait()

    r = lax.broadcasted_iota(jnp.int32, (LANES, LANES), 0)
    c = lax.broadcasted_iota(jnp.int32, (LANES, LANES), 1)
    after = jnp.where(r > c, 1.0, 0.0).astype(BF)
    rowi = lax.broadcasted_iota(jnp.int32, (DEC_ROWS, LANES), 0)
    lane = lax.broadcasted_iota(jnp.int32, (DEC_ROWS, LANES), 1)
    t_row = rowi // SUBLANES
    real_head = rowi % SUBLANES < N_HEADS

    def suffix(l_list):
        nb = len(l_list)
        stack = l_list[0] if nb == 1 else jnp.concatenate(l_list, axis=0)
        inner = _dot_exact(stack, after, left=False)
        out = [None] * nb
        off = jnp.zeros((DEC_ROWS, 1), F32)
        for b in range(nb - 1, -1, -1):
            out[b] = inner[b * DEC_ROWS:(b + 1) * DEC_ROWS] + off
            off = off + jnp.sum(l_list[b], axis=-1, keepdims=True)
        return out, off

    def fox_logits(kf, lf8, is_new):
        nb = len(kf)
        qf = qf_ref[...]
        lf = [jnp.concatenate([x] * (DEC_ROWS // SUBLANES), axis=0) for x in lf8]
        rsum, tot = suffix(lf)
        base = offf_ref[...] - e_ref[...]
        s_list = []
        for b in range(nb):
            s = _dot(qf, kf[b].astype(BF)) + rsum[b] + base
            if is_new:
                s = jnp.where((lane <= t_row) & (lane < tnew), s, NEG_BIG)
            s_list.append(s)
        offf_ref[...] = offf_ref[...] + tot
        m_blk = s_list[0]
        for b in range(1, nb):
            m_blk = jnp.maximum(m_blk, s_list[b])
        m_new = jnp.maximum(m_ref[...], jnp.max(m_blk, axis=-1, keepdims=True))
        needed = jnp.max(jnp.where(real_head, m_blk - m_new, NEG_BIG)) > EXP_ZERO_BELOW
        return s_list, m_new, needed

    def fox_update(s_list, m_new, vf):
        m_old = m_ref[...]
        alpha = jnp.exp(m_old - m_new)
        psum = jnp.zeros((DEC_ROWS, LANES), F32)
        pv = jnp.zeros((DEC_ROWS, BR_WIDTH), F32)
        for b in range(len(s_list)):
            p = jnp.exp(s_list[b] - m_new)
            psum = psum + p
            pv = pv + _dot_nt(p.astype(BF), vf[b].astype(BF))
        m_ref[...] = m_new
        l_ref[...] = l_ref[...] * alpha + jnp.sum(psum, axis=-1, keepdims=True)
        accf_ref[...] = accf_ref[...] * alpha[:, 0:1] + pv

    def sb_part(ks, vs, is_new):
        nb = len(ks)
        qs = qs_ref[...]
        z_list, l_list = [], []
        for b in range(nb):
            z = _dot(qs, ks[b].astype(BF))
            lrest = -_softplus(z)
            if is_new:
                lrest = jnp.where((lane < t_row) & (lane < tnew), lrest, 0.0)
            z_list.append(z)
            l_list.append(lrest)
        later, tot_s = suffix(l_list)
        cs = cs_ref[...]
        av = jnp.zeros((DEC_ROWS, BR_WIDTH), F32)
        for b in range(nb):
            a = jnp.exp(z_list[b] + l_list[b] + later[b] + cs)
            if is_new:
                a = jnp.where((lane < t_row) & (lane < tnew), a, 0.0)
            av = av + _dot_nt(a.astype(BF), vs[b].astype(BF))
        accs_ref[...] = accs_ref[...] + av
        cs_ref[...] = cs + tot_s

    def sb_alive():
        return jnp.max(jnp.where(real_head, cs_ref[...], NEG_BIG)) > EXP_ZERO_BELOW

    for which in range(N_CACHES):
        start(which, 0, 0)

    m_ref[...] = jnp.full(m_ref.shape, NEG_BIG, F32)
    l_ref[...] = jnp.zeros_like(l_ref)
    accf_ref[...] = jnp.zeros_like(accf_ref)
    offf_ref[...] = jnp.zeros_like(offf_ref)
    cs_ref[...] = jnp.zeros_like(cs_ref)
    accs_ref[...] = jnp.zeros_like(accs_ref)
    lfn = jnp.concatenate([lfn_ref[...]] * (DEC_ROWS // SUBLANES), axis=0)
    inner = _dot_exact(lfn, after, left=False)
    e = jnp.sum(jnp.where(lane == t_row, inner, 0.0), axis=-1, keepdims=True)
    e_ref[...] = jnp.broadcast_to(e, e_ref.shape)
    s_new, m_new, _ = fox_logits([knf_ref[...]], [lfn_ref[...]], True)
    fox_update(s_new, m_new, [vnf_ref[...]])
    sb_part([kns_ref[...]], [vns_ref[...]], True)

    def chunk_step(c, carry):
        v_here, sb_here = carry
        slot = c % 2
        more = c + 1 < n_chunks
        pages_of = lambda which: [bufs[which][slot, j] for j in range(npp)]

        wait(KF, c, slot)
        wait(LF, c, slot)

        @pl.when(more)
        def _():
            start(KF, c + 1, 1 - slot)
            start(LF, c + 1, 1 - slot)

        s_list, m_new, needed = fox_logits(pages_of(KF), pages_of(LF), False)

        @pl.when(needed & (v_here == 0))
        def _():
            start(VF, c, slot)

        @pl.when(needed | (v_here == 1))
        def _():
            wait(VF, c, slot)

        v_next = jnp.where(needed & more, 1, 0)

        @pl.when(v_next == 1)
        def _():
            start(VF, c + 1, 1 - slot)

        @pl.when(needed)
        def _():
            fox_update(s_list, m_new, pages_of(VF))

        @pl.when(sb_here == 1)
        def _():
            wait(KS, c, slot)
            wait(VS, c, slot)
            sb_part(pages_of(KS), pages_of(VS), False)

        sb_next = jnp.where((sb_here == 1) & more & sb_alive(), 1, 0)

        @pl.when(sb_next == 1)
        def _():
            start(KS, c + 1, 1 - slot)
            start(VS, c + 1, 1 - slot)

        return v_next, sb_next

    lax.fori_loop(0, n_chunks, chunk_step, (jnp.int32(1), jnp.int32(1)))

    own = (lax.broadcasted_iota(jnp.int32, (DEC_ROWS, BR_WIDTH), 1) // HEAD_DIM
           == lax.broadcasted_iota(jnp.int32, (DEC_ROWS, BR_WIDTH), 0) % SUBLANES)
    nt = DEC_ROWS // SUBLANES
    fo = jnp.where(own, accf_ref[...] / l_ref[:, 0:1], 0.0).reshape(nt, SUBLANES, BR_WIDTH)
    so = jnp.where(own, accs_ref[...], 0.0).reshape(nt, SUBLANES, BR_WIDTH)
    of_ref[...] = jnp.sum(fo, axis=1)
    os_ref[...] = jnp.sum(so, axis=1)


def _dec_attn(page_table, qf, qs, knf, vnf, lfn, kns, vns, ck_f, cv_f, clf, ck_s, cv_s, *, layer, npp, tnew):
    nb, n_pages = page_table.shape
    page = ck_f.shape[3]
    DEC_ROWS = tnew * SUBLANES
    kern = functools.partial(_dec_attn_kernel, layer=layer, npp=npp, n_pages=n_pages, tnew=tnew)
    per_b = lambda shape: pl.BlockSpec((None,) + shape, lambda b, pt: (b, 0, 0))
    in_hbm = pl.BlockSpec(memory_space=pl.ANY)
    in_specs = [per_b((DEC_ROWS, BR_WIDTH)), per_b((DEC_ROWS, BR_WIDTH)),
                per_b((BR_WIDTH, LANES)), per_b((BR_WIDTH, LANES)), per_b((SUBLANES, LANES)),
                per_b((BR_WIDTH, LANES)), per_b((BR_WIDTH, LANES))] + [in_hbm] * N_CACHES
    nt = DEC_ROWS // SUBLANES
    out_spec = pl.BlockSpec((None, nt, BR_WIDTH), lambda b, pt: (b, 0, 0))
    wide = pltpu.VMEM((DEC_ROWS, BR_WIDTH), F32)
    narrow = pltpu.VMEM((DEC_ROWS, LANES), F32)
    kv_slots = pltpu.VMEM((2, npp, BR_WIDTH, page), F32)
    lf_slots = pltpu.VMEM((2, npp, SUBLANES, page), F32)
    grid_spec = pltpu.PrefetchScalarGridSpec(
        num_scalar_prefetch=1,
        grid=(nb,),
        in_specs=in_specs,
        out_specs=[out_spec, out_spec],
        scratch_shapes=[kv_slots, kv_slots, lf_slots, kv_slots, kv_slots, pltpu.SemaphoreType.DMA((N_CACHES, 2)),
                        narrow, narrow, wide, narrow, narrow, narrow, wide],
    )
    return pl.pallas_call(
        kern,
        grid_spec=grid_spec,
        out_shape=[jax.ShapeDtypeStruct((nb, nt, BR_WIDTH), F32)] * 2,
        compiler_params=_cparams(("arbitrary",)),
        name="dec_attn",
    )(page_table, qf, qs, knf, vnf, lfn, kns, vns, ck_f, cv_f, clf, ck_s, cv_s)


def _merge_kernel(x_ref, og_ref, op_ref, of_ref, os_ref, g_ref, wg_ref, wb_ref, wo_ref, o_ref):
    x = x_ref[...]
    d = x.shape[1]
    h = _rms(x, g_ref[...]).astype(BF)
    acc = jnp.zeros(x.shape, F32)
    for b, ref in enumerate((og_ref, op_ref, of_ref, os_ref)):
        gate = jax.nn.sigmoid(_dot_nt(h, wg_ref[b * d:(b + 1) * d, :]))
        acc = acc + gate * _dot(ref[...], wb_ref[b])
    o_ref[...] = x + _dot(acc.astype(BF), wo_ref[...])


def _resident(shape):
    nd = len(shape)
    return pl.BlockSpec(shape, lambda *_: (0,) * nd, pipeline_mode=pl.Buffered(1))


def _merge(x, o_gm, o_pool, o_fox, o_sb, g_mix, w_gate, w_br, w_out, *, layer, tb):
    n, d = x.shape
    tok = lambda w: pl.BlockSpec((tb, w), lambda i: (i, 0))
    gate_spec = pl.BlockSpec((None,) + w_gate.shape[1:], lambda i: (layer, 0, 0), pipeline_mode=pl.Buffered(1))
    return pl.pallas_call(
        _merge_kernel,
        grid=(n // tb,),
        in_specs=[tok(d), tok(BR_WIDTH), tok(BR_WIDTH), tok(BR_WIDTH), tok(BR_WIDTH), _const_spec((1, d)),
                  gate_spec, _resident(w_br.shape), _resident(w_out.shape)],
        out_specs=tok(d),
        out_shape=jax.ShapeDtypeStruct((n, d), F32),
        compiler_params=_cparams(("arbitrary",)),
        name="merge",
    )(x, o_gm, o_pool, o_fox, o_sb, g_mix, w_gate, w_br, w_out)


CONV_HALO = 8


def _gelu_tanh(x):
    return 0.5 * x * (1.0 + jnp.tanh(0.7978845608028654 * (x + 0.044715 * (x * x * x))))


def _ffn_kernel(*refs, tb, seg, nblk_seq, final, with_state, row_chunks):
    (x_ref, pe_ref, gf_ref, wug_ref, wuv_ref, cwg_ref, cwv_ref, cbg_ref, cbv_ref, wd_ref,
     gp_ref, wpg_ref, wpp_ref, gfin_ref) = refs[:14]
    refs = refs[14:]
    if with_state:
        p1g_ref, p1v_ref, p2g_ref, p2v_ref = refs[:4]
        refs = refs[4:]
    o_ref, sg_ref, sv_ref, h_ref, acc_ref, cg_ref, cv_ref = refs
    i = pl.program_id(0)
    f = pl.program_id(1)
    nf = pl.num_programs(1)

    @pl.when(f == 0)
    def _():
        h_ref[...] = _rms(x_ref[...], gf_ref[...]).astype(BF)
        acc_ref[...] = jnp.zeros_like(acc_ref)

    @pl.when(i % nblk_seq == 0)
    def _():
        cg_ref[f] = jnp.zeros((CONV_HALO, cg_ref.shape[2]), F32)
        cv_ref[f] = jnp.zeros((CONV_HALO, cv_ref.shape[2]), F32)

    tf = wug_ref.shape[1]
    rc = tb // row_chunks
    tmod = lax.broadcasted_iota(jnp.int32, (rc, tf), 0) % seg
    head_row = lax.broadcasted_iota(jnp.int32, (CONV_HALO, tf), 0)

    def shifted(a, carry, k):
        moved = pltpu.roll(a, k, 0)
        first = jnp.where(head_row < k, pltpu.roll(carry, k, 0), moved[0:CONV_HALO])
        return jnp.concatenate([first, moved[CONV_HALO:]], axis=0) if rc > CONV_HALO else first

    def conv(a, carry, cw_ref, cb_ref, state, rows):
        prev1 = shifted(a, carry, 1)
        prev2 = shifted(a, carry, 2)
        if with_state:
            prev1 = jnp.where(tmod < 1, state[0][rows, :], prev1)
            prev2 = jnp.where(tmod < 2, state[1][rows, :], prev2)
        cw = cw_ref[...]
        return cb_ref[...] + prev2 * cw[0:1, :] + prev1 * cw[1:2, :] + a * cw[2:3, :]

    carry_g, carry_v = cg_ref[f], cv_ref[f]
    for c in range(row_chunks):
        rows = slice(c * rc, (c + 1) * rc)
        hc = h_ref[rows, :]
        ag = _dot(hc, wug_ref[...])
        av = _dot(hc, wuv_ref[...])
        cgate = conv(ag, carry_g, cwg_ref, cbg_ref, (p1g_ref, p2g_ref) if with_state else None, rows)
        cval = conv(av, carry_v, cwv_ref, cbv_ref, (p1v_ref, p2v_ref) if with_state else None, rows)
        carry_g, carry_v = ag[rc - CONV_HALO:rc, :], av[rc - CONV_HALO:rc, :]
        if with_state:
            sg_ref[rows, :] = ag
            sv_ref[rows, :] = av
        acc_ref[rows, :] += _dot((_gelu_tanh(cgate) * cval).astype(BF), wd_ref[...])
    cg_ref[f] = carry_g
    cv_ref[f] = carry_v
    if not with_state:
        sg_ref[...] = carry_g[CONV_HALO - (CONV_W - 1):CONV_HALO, :]
        sv_ref[...] = carry_v[CONV_HALO - (CONV_W - 1):CONV_HALO, :]

    @pl.when(f == nf - 1)
    def _():
        x2 = x_ref[...] + acc_ref[...]
        gate = jax.nn.sigmoid(_dot(_rms(x2, gp_ref[...]).astype(BF), wpg_ref[...]))
        x3 = x2 + gate * _dot(pe_ref[...].astype(BF), wpp_ref[...])
        if final:
            x3 = _rms(x3, gfin_ref[...])
        o_ref[...] = x3


def _ffn(x, pe, g_ffn, w_up, conv_w, conv_b, w_down, g_ple, w_pg, w_pp, g_final, state, *, tb, seg, tf, final):
    n, d = x.shape
    dff = w_down.shape[0]
    nf = dff // tf
    with_state = state is not None
    nblk = n // tb
    nblk_seq = max(1, seg // tb)
    row_chunks = 1
    kern = functools.partial(_ffn_kernel, tb=tb, seg=seg, nblk_seq=nblk_seq, final=final, with_state=with_state,
                             row_chunks=row_chunks)
    tok = lambda w: pl.BlockSpec((tb, w), lambda i, f: (i, 0))
    cst = lambda shape: pl.BlockSpec(shape, lambda i, f: (0,) * len(shape))
    in_specs = [tok(d), tok(pe.shape[1]), cst((1, d)),
                pl.BlockSpec((d, tf), lambda i, f: (0, f)), pl.BlockSpec((d, tf), lambda i, f: (0, nf + f)),
                pl.BlockSpec((CONV_W, tf), lambda i, f: (0, f)), pl.BlockSpec((CONV_W, tf), lambda i, f: (0, nf + f)),
                pl.BlockSpec((1, tf), lambda i, f: (0, f)), pl.BlockSpec((1, tf), lambda i, f: (0, nf + f)),
                pl.BlockSpec((tf, d), lambda i, f: (f, 0)),
                cst((1, d)), _resident(w_pg.shape), _resident(w_pp.shape), cst((1, d))]
    args = [x, pe, g_ffn, w_up, w_up, conv_w, conv_w, conv_b, conv_b, w_down, g_ple, w_pg, w_pp, g_final]
    if with_state:
        p1, p2 = state
        in_specs += [pl.BlockSpec((tb, tf), lambda i, f: (i, f)), pl.BlockSpec((tb, tf), lambda i, f: (i, nf + f)),
                     pl.BlockSpec((tb, tf), lambda i, f: (i, f)), pl.BlockSpec((tb, tf), lambda i, f: (i, nf + f))]
        args += [p1, p1, p2, p2]
        st_shape = jax.ShapeDtypeStruct((n, dff), F32)
        st_spec = pl.BlockSpec((tb, tf), lambda i, f: (i, f))
    else:
        st_shape = jax.ShapeDtypeStruct((nblk, CONV_W - 1, dff), F32)
        st_spec = pl.BlockSpec((None, CONV_W - 1, tf), lambda i, f: (i, 0, f))
    return pl.pallas_call(
        kern,
        grid=(nblk, nf),
        in_specs=in_specs,
        out_specs=[tok(d), st_spec, st_spec],
        out_shape=[jax.ShapeDtypeStruct((n, d), F32), st_shape, st_shape],
        scratch_shapes=[pltpu.VMEM((tb, d), BF), pltpu.VMEM((tb, d), F32),
                        pltpu.VMEM((nf, CONV_HALO, tf), F32), pltpu.VMEM((nf, CONV_HALO, tf), F32)],
        compiler_params=_cparams(("arbitrary", "arbitrary")),
        name="ffn",
    )(*args)


def _pick_block(n, target):
    b = min(n, target)
    while n % b:
        b //= 2
    return b


def kernel(x_prompt, x_sample, cache_fox_k, cache_fox_v, cache_fox_logf, cache_sb_k, cache_sb_v, state_pool, state_ffn_conv, page_table, p_prompt, p_sample, g_mix, w_in, b_f, gm_ln_g, gm_ln_b, gm_ws, gm_bs, pm_w, pm_scale, w_br, w_out, g_ffn, w_up, conv_w, conv_b, w_down, g_ple, w_ple_gate, w_ple_proj, g_final):
    nbp, t, d = x_prompt.shape
    nbs, ts, _ = x_sample.shape
    depth = w_in.shape[0]
    n_pool, page = cache_fox_k.shape[1], cache_fox_k.shape[2]
    n_pages = page_table.shape[1]
    past = n_pages * page
    dff = w_down.shape[1]
    f2 = 2 * dff
    off_mix = N_BRANCH * d
    off_f = off_mix + 6 * BR_WIDTH
    off_sbq = off_f + N_HEADS
    np_tok, ns_tok = nbp * t, nbs * ts

    tb_p = _pick_block(t, 512)
    tb_ffn = _pick_block(t, 512)
    tq = _pick_block(t, 256)
    tf = dff // 2 if (dff // 2) % LANES == 0 else 256
    npp = _pick_block(n_pages, 8)

    row = lambda a: a.reshape(1, -1)
    xp = x_prompt.reshape(np_tok, d)
    xs = x_sample.reshape(ns_tok, d)
    kv_pages = lambda c: jnp.transpose(c, (0, 1, 3, 4, 2)).reshape(depth, n_pool, BR_WIDTH, page)
    ck_f, cv_f, ck_s, cv_s = kv_pages(cache_fox_k), kv_pages(cache_fox_v), kv_pages(cache_sb_k), kv_pages(cache_sb_v)
    clf = jnp.pad(jnp.swapaxes(cache_fox_logf, 2, 3), ((0, 0), (0, 0), (0, SUBLANES - N_HEADS), (0, 0)))

    gm_tile = lambda a: jnp.tile(a[:, :ts, :ts], (1, nbs, nbs))

    w_t = jnp.transpose(w_in, (2, 0, 1))
    blocks = lambda start, n: [start + W_BLOCK * j for j in range(n // W_BLOCK)]
    mix_starts = (blocks(off_mix, 3 * BR_WIDTH) + blocks(off_mix + 3 * BR_WIDTH, BR_WIDTH) + blocks(off_sbq, BR_WIDTH)
                  + [off_f]
                  + blocks(off_mix + 4 * BR_WIDTH, 2 * BR_WIDTH) + blocks(off_sbq + BR_WIDTH, 2 * BR_WIDTH))
    w_gate_all = _wprep(w_t, blocks(0, off_mix))
    w_mix_all = _wprep(w_t, mix_starts)

    new_p = [[] for _ in range(7)]
    new_s = [[] for _ in range(8)]
    for i in range(depth):
        bf_row = jnp.pad(b_f[i], (0, LANES - N_HEADS)).reshape(1, LANES)
        w_br_b = w_br[i].astype(BF)
        w_out_b = w_out[i].astype(BF)
        w_up_b = w_up[i].astype(BF)
        w_down_b = w_down[i].astype(BF)
        w_pg = w_ple_gate[i].astype(BF)
        w_pp = w_ple_proj[i].astype(BF)
        w_pm_bd = jax.scipy.linalg.block_diag(*[pm_w[i, g] for g in range(len(POOL_WINDOWS))]).astype(BF)
        bs_full = lambda rows: jnp.repeat(jnp.tile(gm_bs[i][:, :min(rows, GM_CHUNK)].T, (rows // min(rows, GM_CHUNK), 1)),
                                          HEAD_DIM, axis=1)
        common = (row(g_mix[i]), w_mix_all, bf_row, row(gm_ln_g[i]), row(gm_ln_b[i]))
        final = i == depth - 1
        heads = lambda a: jnp.transpose(a.reshape(a.shape[0], N_HEADS, HEAD_DIM, a.shape[2]), (0, 3, 1, 2))
        logf = lambda a: jnp.transpose(a[:, :N_HEADS], (0, 2, 1))

        gm_rows = min(GM_CHUNK, tb_p)
        (o_gm, pool_in, fq, sq, _, cum_c, lft, cum_r, fk, fv, sk, sv, fkb, fvb, skb, svb) = _inproj(
            xp, *common, gm_ws[i][:, :gm_rows, :gm_rows], bs_full(gm_rows),
            layer=i, tb=tb_p, seg=t, gm_rows=gm_rows, gm_chunk=GM_CHUNK, nseq=nbp)
        o_pool = _pool(pool_in, w_pm_bd, row(pm_scale[i]), tb=tb_p, rows_per_seq=t, offset=0, pos0=0)
        o_fox = _fox(fq, fkb, fvb, cum_c, cum_r, nseq=nbp, t=t, tq=tq)
        o_sb = _sb(sq, skb, svb, nseq=nbp, t=t, tq=tq)
        x1 = _merge(xp, o_gm, o_pool, o_fox, o_sb, row(g_mix[i]), w_gate_all, w_br_b, w_out_b, layer=i, tb=tb_p)
        xp, st_g, st_v = _ffn(x1, p_prompt[i].reshape(np_tok, -1), row(g_ffn[i]), w_up_b, conv_w[i], row(conv_b[i]),
                              w_down_b, row(g_ple[i]), w_pg, w_pp, row(g_final), None,
                              tb=tb_ffn, seg=t, tf=tf, final=final)
        new_p[0].append(heads(fk))
        new_p[1].append(heads(fv))
        new_p[2].append(logf(lft))
        new_p[3].append(heads(sk))
        new_p[4].append(heads(sv))
        new_p[5].append(pool_in.reshape(nbp, t, BR_WIDTH)[:, t - POOL_BUF:])
        last = slice(t // tb_ffn - 1, None, t // tb_ffn)
        new_p[6].append(jnp.concatenate([st_g[last], st_v[last]], axis=-1))

        (o_gm, pool_in, fq, sq, vn, _, lft, _, fk, fv, sk, sv, _, _, _, _) = _inproj(
            xs, *common, gm_tile(gm_ws[i]), bs_full_sample(gm_bs[i], ts, nbs),
            layer=i, tb=ns_tok, seg=ts, gm_rows=ns_tok, gm_chunk=ts, nseq=1)
        grp_rows = -(-(POOL_BUF + ts) // SUBLANES) * SUBLANES
        lead = grp_rows - POOL_BUF - ts
        full = jnp.concatenate([jnp.zeros((nbs, lead, BR_WIDTH), F32), state_pool[i],
                                pool_in.reshape(nbs, ts, BR_WIDTH)], axis=1)
        o_pool = _pool(full.reshape(nbs * grp_rows, BR_WIDTH), w_pm_bd, row(pm_scale[i]),
                       tb=nbs * grp_rows, rows_per_seq=grp_rows, offset=grp_rows - ts, pos0=past)
        o_pool = o_pool.reshape(nbs, grp_rows, BR_WIDTH)[:, grp_rows - ts:].reshape(ns_tok, BR_WIDTH)
        o_fox, o_sb = _dec_attn(
            page_table, _block_diag_queries(fq, nbs, ts), _block_diag_queries(sq, nbs, ts),
            _pad_new(fk, nbs, ts), _pad_new(fv, nbs, ts),
            jnp.pad(jnp.swapaxes(lft.reshape(SUBLANES, nbs, ts), 0, 1), ((0, 0), (0, 0), (0, LANES - ts))),
            _pad_new(sk, nbs, ts), _pad_new(sv, nbs, ts),
            ck_f, cv_f, clf, ck_s, cv_s, layer=i, npp=npp, tnew=ts)
        o_fox = o_fox.reshape(ns_tok, BR_WIDTH).astype(BF)
        o_sb = o_sb.reshape(ns_tok, BR_WIDTH).astype(BF)
        x1 = _merge(xs, o_gm, o_pool, o_fox, o_sb, row(g_mix[i]), w_gate_all, w_br_b, w_out_b, layer=i, tb=ns_tok)
        buf = state_ffn_conv[i]
        zero = jnp.zeros((nbs, ts - 1, f2), F32)
        p1 = jnp.concatenate([buf[:, 1:2], zero], axis=1).reshape(ns_tok, f2)
        p2 = jnp.concatenate([buf, zero[:, 1:]], axis=1).reshape(ns_tok, f2)
        xs, a_g, a_v = _ffn(x1, p_sample[i].reshape(ns_tok, -1), row(g_ffn[i]), w_up_b, conv_w[i], row(conv_b[i]),
                            w_down_b, row(g_ple[i]), w_pg, w_pp, row(g_final), (p1, p2),
                            tb=ns_tok, seg=ts, tf=tf, final=final)
        new_s[0].append(heads(fk).reshape(nbs, ts, N_HEADS, HEAD_DIM))
        new_s[1].append(heads(fv).reshape(nbs, ts, N_HEADS, HEAD_DIM))
        new_s[2].append(logf(lft).reshape(nbs, ts, N_HEADS))
        new_s[3].append(heads(sk).reshape(nbs, ts, N_HEADS, HEAD_DIM))
        new_s[4].append(heads(sv).reshape(nbs, ts, N_HEADS, HEAD_DIM))
        new_s[5].append(jnp.concatenate([state_pool[i], pool_in.reshape(nbs, ts, BR_WIDTH)], axis=1)[:, ts:])
        a_full = jnp.concatenate([a_g, a_v], axis=-1).reshape(nbs, ts, f2)
        new_s[6].append(jnp.concatenate([buf, a_full], axis=1)[:, ts:])
        new_s[7].append(vn.reshape(nbs, ts, BR_WIDTH))

    stk = lambda lst: jnp.stack(lst, axis=0)
    return (xp.reshape(nbp, t, d), xs.reshape(nbs, ts, d),
            *[stk(a) for a in new_p], *[stk(a) for a in new_s])


def bs_full_sample(gm_bs_i, ts, nbs):
    return jnp.repeat(jnp.tile(gm_bs_i[:, :ts].T, (nbs, 1)), HEAD_DIM, axis=1)


def _block_diag_queries(q, nbs, ts):
    q = q.reshape(nbs, ts, 1, N_HEADS, HEAD_DIM)
    eye = jnp.eye(SUBLANES, N_HEADS, dtype=q.dtype).reshape(1, 1, SUBLANES, N_HEADS, 1)
    return (q * eye).reshape(nbs, ts * SUBLANES, BR_WIDTH)


def _pad_new(a, nbs, ts):
    return jnp.pad(jnp.swapaxes(a.reshape(BR_WIDTH, nbs, ts), 0, 1), ((0, 0), (0, 0), (0, LANES - ts)))
```

```python
import functools

import jax
import jax.numpy as jnp
from jax import lax
from jax.experimental import pallas as pl
from jax.experimental.pallas import tpu as pltpu

F32 = jnp.float32
BF = jnp.bfloat16

EPS = 1e-6
N_BRANCH = 4
BR_WIDTH = 256
N_HEADS = 4
HEAD_DIM = 64
LANES = 128
SUBLANES = 8
POOL_WINDOWS = (2, 4, 8, 16)
POOL_BUF = 15
POOL_HALO = 16
CONV_W = 3
GM_CHUNK = 128
NEG_BIG = -1e30
EXP_ZERO_BELOW = -104.0
VMEM_LIMIT = 48 * 1024 * 1024

C_U, C_V, C_POOL, C_FQ, C_SQ, C_F = [BR_WIDTH * i for i in range(6)]
N_TOKMAJ = C_F + LANES
R_FK, R_FV, R_SK, R_SV = [BR_WIDTH * i for i in range(4)]
N_HEADMAJ = 4 * BR_WIDTH
W_BLOCK = 128
HEAD_GROUP = 4


def _cparams(sem):
    return pltpu.CompilerParams(dimension_semantics=sem, vmem_limit_bytes=VMEM_LIMIT)


def _rms(x, g):
    return x * lax.rsqrt(jnp.mean(x * x, axis=-1, keepdims=True) + EPS) * g


def _softplus(x):
    return jnp.maximum(x, 0.0) + jnp.log1p(jnp.exp(-jnp.abs(x)))


def _split3(x):
    hi = x.astype(BF)
    r = x - hi.astype(F32)
    mid = r.astype(BF)
    r = r - mid.astype(F32)
    return hi, mid, r.astype(BF)


def _dot(a, b):
    return jnp.dot(a, b, preferred_element_type=F32)


def _dot_nt(a, b):
    return lax.dot_general(a, b, (((1,), (1,)), ((), ())), preferred_element_type=F32)


def _dot_exact(x, ones_bf, left, terms=3):
    parts = _split3(x)[:terms]
    out = None
    for p in parts:
        d = _dot(ones_bf, p) if left else _dot(p, ones_bf)
        out = d if out is None else out + d
    return out


def _const_spec(shape):
    nd = len(shape)
    return pl.BlockSpec(shape, lambda *_: (0,) * nd)


def _wprep_kernel(st_ref, w_ref, o_ref):
    del st_ref
    for layer in range(o_ref.shape[0]):
        o_ref[layer] = w_ref[:, layer, :].astype(BF)


def _wprep(w_t, starts):
    n_out, depth, d = w_t.shape
    nblk = len(starts)
    grid_spec = pltpu.PrefetchScalarGridSpec(
        num_scalar_prefetch=1,
        grid=(nblk,),
        in_specs=[pl.BlockSpec((pl.Element(W_BLOCK), pl.Element(depth), pl.Element(d)), lambda j, st: (st[j], 0, 0))],
        out_specs=pl.BlockSpec((depth, W_BLOCK, d), lambda j, st: (0, j, 0)),
    )
    return pl.pallas_call(
        _wprep_kernel,
        grid_spec=grid_spec,
        out_shape=jax.ShapeDtypeStruct((depth, nblk * W_BLOCK, d), BF),
        compiler_params=_cparams(("arbitrary",)),
        name="wprep",
    )(jnp.asarray(starts, jnp.int32), w_t)


def _inproj_kernel(x_ref, g_ref, w_ref, bfr_ref, lng_ref, lnb_ref, ws_ref, bs_ref,
                   ogm_ref, pool_ref, fq_ref, sq_ref, vn_ref, cc_ref, lft_ref, cr_ref,
                   fk_ref, fv_ref, sk_ref, sv_ref, fkb_ref, fvb_ref, skb_ref, svb_ref,
                   carc_ref, *, tb, seg, gm_rows, gm_chunk):
    i = pl.program_id(0)
    h = _rms(x_ref[...], g_ref[...]).astype(BF)
    z = _dot_nt(h, w_ref[0:N_TOKMAJ, :])
    zt = _dot_nt(w_ref[N_TOKMAJ:N_TOKMAJ + N_HEADMAJ, :], h)

    r = lax.broadcasted_iota(jnp.int32, (gm_rows, gm_rows), 0)
    c = lax.broadcasted_iota(jnp.int32, (gm_rows, gm_rows), 1)
    mix_mask = (r >= c) & ((r // gm_chunk) == (c // gm_chunk))
    grp = lax.broadcasted_iota(jnp.int32, (gm_rows, BR_WIDTH), 1) // HEAD_DIM
    for blk in range(tb // gm_rows):
        rows = slice(blk * gm_rows, (blk + 1) * gm_rows)
        u = z[rows, C_U:C_U + BR_WIDTH]
        v = z[rows, C_V:C_V + BR_WIDTH]
        mu = jnp.mean(v, axis=-1, keepdims=True)
        var = jnp.mean(jnp.square(v - mu), axis=-1, keepdims=True)
        vn = (v - mu) * lax.rsqrt(var + EPS) * lng_ref[...] + lnb_ref[...]
        vn_ref[rows, :] = vn
        vnb = vn.astype(BF)
        s = bs_ref[...]
        for g in range(N_HEADS):
            wg = jnp.where(mix_mask, ws_ref[g], 0.0).astype(BF)
            s = s + jnp.where(grp == g, _dot(wg, vnb), 0.0)
        ogm_ref[rows, :] = (u * s).astype(BF)

    pool_ref[...] = z[:, C_POOL:C_POOL + BR_WIDTH]
    scale = HEAD_DIM ** -0.5
    fq_ref[...] = (z[:, C_FQ:C_FQ + BR_WIDTH] * scale).astype(BF)
    sq_ref[...] = (z[:, C_SQ:C_SQ + BR_WIDTH] * scale).astype(BF)
    for start, full_ref, half_ref in ((R_FK, fk_ref, fkb_ref), (R_FV, fv_ref, fvb_ref),
                                      (R_SK, sk_ref, skb_ref), (R_SV, sv_ref, svb_ref)):
        part = zt[start:start + BR_WIDTH, :]
        full_ref[...] = part
        half_ref[...] = part.astype(BF)

    lf = -_softplus(-(z[:, C_F:C_F + LANES] + bfr_ref[...]))
    lft_ref[...] = lf.T[0:SUBLANES, :]
    r = lax.broadcasted_iota(jnp.int32, (tb, tb), 0)
    c = lax.broadcasted_iota(jnp.int32, (tb, tb), 1)
    if seg >= tb:
        lower = (r >= c)
        nblk_seq = seg // tb

        @pl.when(i % nblk_seq == 0)
        def _():
            carc_ref[...] = jnp.zeros_like(carc_ref)
    else:
        lower = (r >= c) & ((r // seg) == (c // seg))
        carc_ref[...] = jnp.zeros_like(carc_ref)
    cum_c = _dot_exact(lf, jnp.where(lower, 1.0, 0.0).astype(BF), left=True) + carc_ref[0:1, :]
    cc_ref[...] = cum_c
    cr_ref[...] = cum_c.T[0:SUBLANES, :]
    carc_ref[...] = jnp.broadcast_to(cum_c[tb - 1:tb, :], carc_ref.shape)


def _inproj(x, g_mix, w_mix, bf_row, ln_g, ln_b, ws, bs, *, layer, tb, seg, gm_rows, gm_chunk, nseq):
    n, d = x.shape
    kern = functools.partial(_inproj_kernel, tb=tb, seg=seg, gm_rows=gm_rows, gm_chunk=gm_chunk)
    tok = lambda w: pl.BlockSpec((tb, w), lambda i: (i, 0))
    per_seq = n // nseq
    nblk_seq = per_seq // tb
    chan = lambda rows: pl.BlockSpec((None, rows, tb), lambda i: (i // nblk_seq, 0, i % nblk_seq))
    tokmaj = lambda width, dt: jax.ShapeDtypeStruct((n, width), dt)
    chanmaj = lambda rows, dt: jax.ShapeDtypeStruct((nseq, rows, per_seq), dt)
    out_shape = [
        tokmaj(BR_WIDTH, BF),
        tokmaj(BR_WIDTH, F32),
        tokmaj(BR_WIDTH, BF),
        tokmaj(BR_WIDTH, BF),
        tokmaj(BR_WIDTH, F32),
        tokmaj(LANES, F32),
        chanmaj(SUBLANES, F32),
        chanmaj(SUBLANES, F32),
    ] + [chanmaj(BR_WIDTH, F32)] * 4 + [chanmaj(BR_WIDTH, BF)] * 4
    out_specs = ([tok(BR_WIDTH)] * 5 + [tok(LANES), chan(SUBLANES), chan(SUBLANES)] + [chan(BR_WIDTH)] * 8)
    w_rows = w_mix.shape[1]
    return pl.pallas_call(
        kern,
        grid=(n // tb,),
        in_specs=[tok(d), _const_spec((1, d)),
                  pl.BlockSpec((None, w_rows, d), lambda i: (layer, 0, 0), pipeline_mode=pl.Buffered(1)),
                  _const_spec(bf_row.shape), _const_spec(ln_g.shape),
                  _const_spec(ln_b.shape), _const_spec(ws.shape), _const_spec(bs.shape)],
        out_specs=out_specs,
        out_shape=out_shape,
        scratch_shapes=[pltpu.VMEM((SUBLANES, LANES), F32)],
        compiler_params=_cparams(("arbitrary",)),
        name="inproj",
    )(x, g_mix, w_mix, bf_row, ln_g, ln_b, ws, bs)


def _pool_kernel(x_ref, w_ref, sc_ref, o_ref, ext_ref, *, tb, rows_per_seq, offset, pos0, nblk_seq):
    i = pl.program_id(0)

    @pl.when(i % nblk_seq == 0)
    def _():
        ext_ref[0:POOL_HALO, :] = jnp.zeros((POOL_HALO, BR_WIDTH), F32)

    ext_ref[POOL_HALO:POOL_HALO + tb, :] = x_ref[...]
    rowi = lax.broadcasted_iota(jnp.int32, (tb, LANES), 0)
    lane = lax.broadcasted_iota(jnp.int32, (tb, LANES), 1)
    pos = pos0 + (i * tb + rowi) % rows_per_seq - offset
    first = lane < HEAD_DIM
    halves = []
    for half in range(2):
        cols = slice(half * LANES, (half + 1) * LANES)
        shifted = lambda j: ext_ref[POOL_HALO - j:POOL_HALO - j + tb, cols]
        w_small, w_big = POOL_WINDOWS[2 * half], POOL_WINDOWS[2 * half + 1]
        cur = shifted(0)
        acc = cur
        for j in range(1, w_small):
            acc = acc + shifted(j)
        small = acc
        for j in range(w_small, w_big):
            acc = acc + shifted(j)
        win = jnp.where(first, small, acc)
        width = jnp.where(first, w_small, w_big)
        cnt = jnp.clip(pos + 1, 1, width).astype(F32)
        halves.append(win / cnt - cur)
    d = jnp.concatenate(halves, axis=1).astype(BF)
    o_ref[...] = (_dot(d, w_ref[...]) * sc_ref[...]).astype(BF)
    ext_ref[0:POOL_HALO, :] = ext_ref[tb:tb + POOL_HALO, :]


def _pool(x, w_bd, scale, *, tb, rows_per_seq, offset, pos0):
    n = x.shape[0]
    kern = functools.partial(_pool_kernel, tb=tb, rows_per_seq=rows_per_seq, offset=offset, pos0=pos0,
                             nblk_seq=max(1, rows_per_seq // tb))
    return pl.pallas_call(
        kern,
        grid=(n // tb,),
        in_specs=[pl.BlockSpec((tb, BR_WIDTH), lambda i: (i, 0)), _const_spec(w_bd.shape), _const_spec(scale.shape)],
        out_specs=pl.BlockSpec((tb, BR_WIDTH), lambda i: (i, 0)),
        out_shape=jax.ShapeDtypeStruct((n, BR_WIDTH), BF),
        scratch_shapes=[pltpu.VMEM((tb + POOL_HALO, BR_WIDTH), F32)],
        compiler_params=_cparams(("arbitrary",)),
        name="pool",
    )(x, w_bd, scale)


def _head_queries(q):
    left = lax.broadcasted_iota(jnp.int32, (q.shape[0], LANES), 1) < HEAD_DIM
    out = []
    for h in range(N_HEADS):
        pair = q[:, (h // 2) * LANES:(h // 2 + 1) * LANES]
        keep = left if h % 2 == 0 else jnp.logical_not(left)
        out.append(jnp.where(keep, pair, jnp.zeros_like(pair)))
    return out, left


def _fox_kernel(q_ref, k_ref, v_ref, cc_ref, cr_ref, o_ref, kn_ref, *, tq):
    qb = pl.program_id(1)
    qh, left = _head_queries(q_ref[...])
    cc = cc_ref[...]
    cq = [cc[:, h:h + 1] for h in range(N_HEADS)]
    r = lax.broadcasted_iota(jnp.int32, (tq, tq), 0)
    c = lax.broadcasted_iota(jnp.int32, (tq, tq), 1)
    causal = c <= r

    @pl.when(qb == 0)
    def _():
        hr = lax.broadcasted_iota(jnp.int32, (SUBLANES, BR_WIDTH), 0)
        hc = lax.broadcasted_iota(jnp.int32, (SUBLANES, BR_WIDTH), 1) // HEAD_DIM
        head_rows = jnp.where(hr == hc, 1.0, 0.0).astype(BF)

        def chunk(ci, best):
            kc = k_ref[:, pl.ds(pl.multiple_of(ci * tq, tq), tq)].astype(F32)
            n2 = _dot_exact(kc * kc, head_rows, left=True)
            return jnp.maximum(best, jnp.max(n2, axis=1, keepdims=True))

        best = lax.fori_loop(0, k_ref.shape[1] // tq, chunk, jnp.zeros((SUBLANES, 1), F32))
        kn_ref[...] = jnp.broadcast_to(best, kn_ref.shape)

    reach = []
    for h in range(N_HEADS):
        qf = qh[h].astype(F32)
        qn2 = jnp.sum(qf * qf, axis=-1, keepdims=True)
        reach.append(jnp.sqrt(qn2 * kn_ref[h:h + 1, 0:1]) * 1.001 + 0.01 + cq[h])

    def block(j, carry, masked):
        ms, ls, accs = carry
        start = pl.multiple_of(j * tq, tq)
        kblk = k_ref[:, pl.ds(start, tq)]
        vblk = v_ref[:, pl.ds(start, tq)]
        pairs = [slice((h // 2) * LANES, (h // 2 + 1) * LANES) for h in range(N_HEADS)]
        new_m, new_l, alphas, pv = ([None] * N_HEADS for _ in range(4))
        for g in range(0, N_HEADS, HEAD_GROUP):
            heads = range(g, g + HEAD_GROUP)
            ss = {h: _dot(qh[h], kblk[pairs[h], :]) + (cq[h] - cr_ref[h:h + 1, pl.ds(start, tq)]) for h in heads}
            if masked:
                ss = {h: jnp.where(causal, ss[h], NEG_BIG) for h in heads}
            for h in heads:
                new_m[h] = jnp.maximum(ms[h], jnp.max(ss[h], axis=-1, keepdims=True))
            for h in heads:
                alphas[h] = jnp.exp(ms[h] - new_m[h])
            ps = {h: jnp.exp(ss[h] - new_m[h]) for h in heads}
            for h in heads:
                new_l[h] = ls[h] * alphas[h] + jnp.sum(ps[h], axis=-1, keepdims=True)
            for h in heads:
                pv[h] = _dot_nt(ps[h].astype(BF), vblk[pairs[h], :])
        new_acc = []
        for hp in range(N_HEADS // 2):
            a = jnp.where(left, alphas[2 * hp], alphas[2 * hp + 1])
            new_acc.append(accs[hp] * a + jnp.where(left, pv[2 * hp], pv[2 * hp + 1]))
        return tuple(new_m), tuple(new_l), tuple(new_acc)

    init = (tuple(jnp.full((tq, 1), NEG_BIG, F32) for _ in range(N_HEADS)),
            tuple(jnp.zeros((tq, 1), F32) for _ in range(N_HEADS)),
            tuple(jnp.zeros((tq, LANES), F32) for _ in range(N_HEADS // 2)))
    def live(state):
        j, ms = state[0], state[1]
        start = pl.multiple_of(jnp.maximum(j, 0) * tq, tq)
        gap = None
        for h in range(N_HEADS):
            ck_last = cr_ref[h:h + 1, pl.ds(start, tq)][:, tq - 1:tq]
            g = reach[h] - ck_last - ms[h]
            gap = g if gap is None else jnp.maximum(gap, g)
        return (j >= 0) & (jnp.max(gap) > EXP_ZERO_BELOW)

    def step(state):
        j = state[0]
        return (j - 1,) + block(j, state[1:], False)

    first = block(qb, init, True)
    _, ms, ls, accs = lax.while_loop(live, step, (qb - 1,) + first)
    for hp in range(N_HEADS // 2):
        den = jnp.where(left, ls[2 * hp], ls[2 * hp + 1])
        o_ref[:, hp * LANES:(hp + 1) * LANES] = (accs[hp] / den).astype(BF)


def _fox(q, kb, vb, cum_c, cum_r, *, nseq, t, tq):
    kern = functools.partial(_fox_kernel, tq=tq)
    nq = t // tq
    return pl.pallas_call(
        kern,
        grid=(nseq, nq),
        in_specs=[pl.BlockSpec((tq, BR_WIDTH), lambda n, i: (n * nq + i, 0)),
                  pl.BlockSpec((None, BR_WIDTH, t), lambda n, i: (n, 0, 0)),
                  pl.BlockSpec((None, BR_WIDTH, t), lambda n, i: (n, 0, 0)),
                  pl.BlockSpec((tq, LANES), lambda n, i: (n * nq + i, 0)),
                  pl.BlockSpec((None, SUBLANES, t), lambda n, i: (n, 0, 0))],
        out_specs=pl.BlockSpec((tq, BR_WIDTH), lambda n, i: (n * nq + i, 0)),
        out_shape=jax.ShapeDtypeStruct((nseq * t, BR_WIDTH), BF),
        scratch_shapes=[pltpu.VMEM((SUBLANES, LANES), F32)],
        compiler_params=_cparams(("arbitrary", "arbitrary")),
        name="fox_prompt",
    )(q, kb, vb, cum_c, cum_r)


def _sb_kernel(q_ref, k_ref, v_ref, o_ref, *, tq):
    qb = pl.program_id(1)
    qh, left = _head_queries(q_ref[...])
    r = lax.broadcasted_iota(jnp.int32, (tq, tq), 0)
    c = lax.broadcasted_iota(jnp.int32, (tq, tq), 1)
    strict = c < r
    after = jnp.where(r > c, 1.0, 0.0).astype(BF)

    def block(j, carry, masked):
        cs, accs = carry
        start = pl.multiple_of(j * tq, tq)
        kblk = k_ref[:, pl.ds(start, tq)]
        vblk = v_ref[:, pl.ds(start, tq)]
        pairs = [slice((h // 2) * LANES, (h // 2 + 1) * LANES) for h in range(N_HEADS)]
        zs = [_dot(qh[h], kblk[pairs[h], :]) for h in range(N_HEADS)]
        lrests = [-_softplus(z) for z in zs]
        if masked:
            lrests = [jnp.where(strict, l, 0.0) for l in lrests]
        laters = [_dot_exact(lrests[h], after, left=False, terms=2) + cs[h] for h in range(N_HEADS)]
        weights = [jnp.exp(zs[h] + lrests[h] + laters[h]) for h in range(N_HEADS)]
        if masked:
            weights = [jnp.where(strict, a, 0.0) for a in weights]
        av = [_dot_nt(weights[h].astype(BF), vblk[pairs[h], :]) for h in range(N_HEADS)]
        new_c = [cs[h] + jnp.sum(lrests[h], axis=-1, keepdims=True) for h in range(N_HEADS)]
        new_acc = tuple(accs[hp] + jnp.where(left, av[2 * hp], av[2 * hp + 1]) for hp in range(N_HEADS // 2))
        return tuple(new_c), new_acc

    init = (tuple(jnp.zeros((tq, 1), F32) for _ in range(N_HEADS)),
            tuple(jnp.zeros((tq, LANES), F32) for _ in range(N_HEADS // 2)))
    def live(state):
        j, cs = state[0], state[1]
        top = cs[0]
        for h in range(1, N_HEADS):
            top = jnp.maximum(top, cs[h])
        return (j >= 0) & (jnp.max(top) > EXP_ZERO_BELOW)

    def step(state):
        j = state[0]
        return (j - 1,) + block(j, state[1:], False)

    first = block(qb, init, True)
    _, _, accs = lax.while_loop(live, step, (qb - 1,) + first)
    for hp in range(N_HEADS // 2):
        o_ref[:, hp * LANES:(hp + 1) * LANES] = accs[hp].astype(BF)


def _sb(q, kb, vb, *, nseq, t, tq):
    kern = functools.partial(_sb_kernel, tq=tq)
    nq = t // tq
    return pl.pallas_call(
        kern,
        grid=(nseq, nq),
        in_specs=[pl.BlockSpec((tq, BR_WIDTH), lambda n, i: (n * nq + i, 0)),
                  pl.BlockSpec((None, BR_WIDTH, t), lambda n, i: (n, 0, 0)),
                  pl.BlockSpec((None, BR_WIDTH, t), lambda n, i: (n, 0, 0))],
        out_specs=pl.BlockSpec((tq, BR_WIDTH), lambda n, i: (n * nq + i, 0)),
        out_shape=jax.ShapeDtypeStruct((nseq * t, BR_WIDTH), BF),
        compiler_params=_cparams(("arbitrary", "arbitrary")),
        name="sb_prompt",
    )(q, kb, vb)


N_CACHES = 5
KF, VF, LF, KS, VS = range(N_CACHES)


def _dec_attn_kernel(pt_ref, qf_ref, qs_ref, knf_ref, vnf_ref, lfn_ref, kns_ref, vns_ref,
                     ckf_hbm, cvf_hbm, clf_hbm, cks_hbm, cvs_hbm, of_ref, os_ref,
                     kf_buf, vf_buf, lf_buf, ks_buf, vs_buf, sems,
                     m_ref, l_ref, accf_ref, offf_ref, e_ref, cs_ref, accs_ref,
                     *, layer, npp, n_pages, tnew):
    DEC_ROWS = tnew * SUBLANES
    b_id = pl.program_id(0)
    n_chunks = n_pages // npp
    hbm = (ckf_hbm, cvf_hbm, clf_hbm, cks_hbm, cvs_hbm)
    bufs = (kf_buf, vf_buf, lf_buf, ks_buf, vs_buf)

    def chunk_copies(which, chunk, slot):
        first = n_pages - (chunk + 1) * npp
        return [pltpu.make_async_copy(hbm[which].at[layer, pt_ref[b_id, first + j]],
                                      bufs[which].at[slot, j], sems.at[which, slot]) for j in range(npp)]

    def start(which, chunk, slot):
        for cp in chunk_copies(which, chunk, slot):
            cp.start()

    def wait(which, chunk, slot):
        for cp in chunk_copies(which, chunk, slot):
            cp.wait()

    r = lax.broadcasted_iota(jnp.int32, (LANES, LANES), 0)
    c = lax.broadcasted_iota(jnp.int32, (LANES, LANES), 1)
    after = jnp.where(r > c, 1.0, 0.0).astype(BF)
    rowi = lax.broadcasted_iota(jnp.int32, (DEC_ROWS, LANES), 0)
    lane = lax.broadcasted_iota(jnp.int32, (DEC_ROWS, LANES), 1)
    t_row = rowi // SUBLANES
    real_head = rowi % SUBLANES < N_HEADS

    def suffix(l_list):
        nb = len(l_list)
        stack = l_list[0] if nb == 1 else jnp.concatenate(l_list, axis=0)
        inner = _dot_exact(stack, after, left=False)
        out = [None] * nb
        off = jnp.zeros((DEC_ROWS, 1), F32)
        for b in range(nb - 1, -1, -1):
            out[b] = inner[b * DEC_ROWS:(b + 1) * DEC_ROWS] + off
            off = off + jnp.sum(l_list[b], axis=-1, keepdims=True)
        return out, off

    def fox_logits(kf, lf8, is_new):
        nb = len(kf)
        qf = qf_ref[...]
        lf = [jnp.concatenate([x] * (DEC_ROWS // SUBLANES), axis=0) for x in lf8]
        rsum, tot = suffix(lf)
        base = offf_ref[...] - e_ref[...]
        s_list = []
        for b in range(nb):
            s = _dot(qf, kf[b].astype(BF)) + rsum[b] + base
            if is_new:
                s = jnp.where((lane <= t_row) & (lane < tnew), s, NEG_BIG)
            s_list.append(s)
        offf_ref[...] = offf_ref[...] + tot
        m_blk = s_list[0]
        for b in range(1, nb):
            m_blk = jnp.maximum(m_blk, s_list[b])
        m_new = jnp.maximum(m_ref[...], jnp.max(m_blk, axis=-1, keepdims=True))
        needed = jnp.max(jnp.where(real_head, m_blk - m_new, NEG_BIG)) > EXP_ZERO_BELOW
        return s_list, m_new, needed

    def fox_update(s_list, m_new, vf):
        m_old = m_ref[...]
        alpha = jnp.exp(m_old - m_new)
        psum = jnp.zeros((DEC_ROWS, LANES), F32)
        pv = jnp.zeros((DEC_ROWS, BR_WIDTH), F32)
        for b in range(len(s_list)):
            p = jnp.exp(s_list[b] - m_new)
            psum = psum + p
            pv = pv + _dot_nt(p.astype(BF), vf[b].astype(BF))
        m_ref[...] = m_new
        l_ref[...] = l_ref[...] * alpha + jnp.sum(psum, axis=-1, keepdims=True)
        accf_ref[...] = accf_ref[...] * alpha[:, 0:1] + pv

    def sb_part(ks, vs, is_new):
        nb = len(ks)
        qs = qs_ref[...]
        z_list, l_list = [], []
        for b in range(nb):
            z = _dot(qs, ks[b].astype(BF))
            lrest = -_softplus(z)
            if is_new:
                lrest = jnp.where((lane < t_row) & (lane < tnew), lrest, 0.0)
            z_list.append(z)
            l_list.append(lrest)
        later, tot_s = suffix(l_list)
        cs = cs_ref[...]
        av = jnp.zeros((DEC_ROWS, BR_WIDTH), F32)
        for b in range(nb):
            a = jnp.exp(z_list[b] + l_list[b] + later[b] + cs)
            if is_new:
                a = jnp.where((lane < t_row) & (lane < tnew), a, 0.0)
            av = av + _dot_nt(a.astype(BF), vs[b].astype(BF))
        accs_ref[...] = accs_ref[...] + av
        cs_ref[...] = cs + tot_s

    def sb_alive():
        return jnp.max(jnp.where(real_head, cs_ref[...], NEG_BIG)) > EXP_ZERO_BELOW

    for which in range(N_CACHES):
        start(which, 0, 0)

    m_ref[...] = jnp.full(m_ref.shape, NEG_BIG, F32)
    l_ref[...] = jnp.zeros_like(l_ref)
    accf_ref[...] = jnp.zeros_like(accf_ref)
    offf_ref[...] = jnp.zeros_like(offf_ref)
    cs_ref[...] = jnp.zeros_like(cs_ref)
    accs_ref[...] = jnp.zeros_like(accs_ref)
    lfn = jnp.concatenate([lfn_ref[...]] * (DEC_ROWS // SUBLANES), axis=0)
    inner = _dot_exact(lfn, after, left=False)
    e = jnp.sum(jnp.where(lane == t_row, inner, 0.0), axis=-1, keepdims=True)
    e_ref[...] = jnp.broadcast_to(e, e_ref.shape)
    s_new, m_new, _ = fox_logits([knf_ref[...]], [lfn_ref[...]], True)
    fox_update(s_new, m_new, [vnf_ref[...]])
    sb_part([kns_ref[...]], [vns_ref[...]], True)

    def chunk_step(c, carry):
        v_here, sb_here = carry
        slot = c % 2
        more = c + 1 < n_chunks
        pages_of = lambda which: [bufs[which][slot, j] for j in range(npp)]

        wait(KF, c, slot)
        wait(LF, c, slot)

        @pl.when(more)
        def _():
            start(KF, c + 1, 1 - slot)
            start(LF, c + 1, 1 - slot)

        s_list, m_new, needed = fox_logits(pages_of(KF), pages_of(LF), False)

        @pl.when(needed & (v_here == 0))
        def _():
            start(VF, c, slot)

        @pl.when(needed | (v_here == 1))
        def _():
            wait(VF, c, slot)

        v_next = jnp.where(needed & more, 1, 0)

        @pl.when(v_next == 1)
        def _():
            start(VF, c + 1, 1 - slot)

        @pl.when(needed)
        def _():
            fox_update(s_list, m_new, pages_of(VF))

        @pl.when(sb_here == 1)
        def _():
            wait(KS, c, slot)
            wait(VS, c, slot)
            sb_part(pages_of(KS), pages_of(VS), False)

        sb_next = jnp.where((sb_here == 1) & more & sb_alive(), 1, 0)

        @pl.when(sb_next == 1)
        def _():
            start(KS, c + 1, 1 - slot)
            start(VS, c + 1, 1 - slot)

        return v_next, sb_next

    lax.fori_loop(0, n_chunks, chunk_step, (jnp.int32(1), jnp.int32(1)))

    own = (lax.broadcasted_iota(jnp.int32, (DEC_ROWS, BR_WIDTH), 1) // HEAD_DIM
           == lax.broadcasted_iota(jnp.int32, (DEC_ROWS, BR_WIDTH), 0) % SUBLANES)
    nt = DEC_ROWS // SUBLANES
    fo = jnp.where(own, accf_ref[...] / l_ref[:, 0:1], 0.0).reshape(nt, SUBLANES, BR_WIDTH)
    so = jnp.where(own, accs_ref[...], 0.0).reshape(nt, SUBLANES, BR_WIDTH)
    of_ref[...] = jnp.sum(fo, axis=1)
    os_ref[...] = jnp.sum(so, axis=1)


def _dec_attn(page_table, qf, qs, knf, vnf, lfn, kns, vns, ck_f, cv_f, clf, ck_s, cv_s, *, layer, npp, tnew):
    nb, n_pages = page_table.shape
    page = ck_f.shape[3]
    DEC_ROWS = tnew * SUBLANES
    kern = functools.partial(_dec_attn_kernel, layer=layer, npp=npp, n_pages=n_pages, tnew=tnew)
    per_b = lambda shape: pl.BlockSpec((None,) + shape, lambda b, pt: (b, 0, 0))
    in_hbm = pl.BlockSpec(memory_space=pl.ANY)
    in_specs = [per_b((DEC_ROWS, BR_WIDTH)), per_b((DEC_ROWS, BR_WIDTH)),
                per_b((BR_WIDTH, LANES)), per_b((BR_WIDTH, LANES)), per_b((SUBLANES, LANES)),
                per_b((BR_WIDTH, LANES)), per_b((BR_WIDTH, LANES))] + [in_hbm] * N_CACHES
    nt = DEC_ROWS // SUBLANES
    out_spec = pl.BlockSpec((None, nt, BR_WIDTH), lambda b, pt: (b, 0, 0))
    wide = pltpu.VMEM((DEC_ROWS, BR_WIDTH), F32)
    narrow = pltpu.VMEM((DEC_ROWS, LANES), F32)
    kv_slots = pltpu.VMEM((2, npp, BR_WIDTH, page), F32)
    lf_slots = pltpu.VMEM((2, npp, SUBLANES, page), F32)
    grid_spec = pltpu.PrefetchScalarGridSpec(
        num_scalar_prefetch=1,
        grid=(nb,),
        in_specs=in_specs,
        out_specs=[out_spec, out_spec],
        scratch_shapes=[kv_slots, kv_slots, lf_slots, kv_slots, kv_slots, pltpu.SemaphoreType.DMA((N_CACHES, 2)),
                        narrow, narrow, wide, narrow, narrow, narrow, wide],
    )
    return pl.pallas_call(
        kern,
        grid_spec=grid_spec,
        out_shape=[jax.ShapeDtypeStruct((nb, nt, BR_WIDTH), F32)] * 2,
        compiler_params=_cparams(("arbitrary",)),
        name="dec_attn",
    )(page_table, qf, qs, knf, vnf, lfn, kns, vns, ck_f, cv_f, clf, ck_s, cv_s)


def _merge_kernel(x_ref, og_ref, op_ref, of_ref, os_ref, g_ref, wg_ref, wb_ref, wo_ref, o_ref):
    x = x_ref[...]
    d = x.shape[1]
    h = _rms(x, g_ref[...]).astype(BF)
    acc = jnp.zeros(x.shape, F32)
    for b, ref in enumerate((og_ref, op_ref, of_ref, os_ref)):
        gate = jax.nn.sigmoid(_dot_nt(h, wg_ref[b * d:(b + 1) * d, :]))
        acc = acc + gate * _dot(ref[...], wb_ref[b])
    o_ref[...] = x + _dot(acc.astype(BF), wo_ref[...])


def _resident(shape):
    nd = len(shape)
    return pl.BlockSpec(shape, lambda *_: (0,) * nd, pipeline_mode=pl.Buffered(1))


def _merge(x, o_gm, o_pool, o_fox, o_sb, g_mix, w_gate, w_br, w_out, *, layer, tb):
    n, d = x.shape
    tok = lambda w: pl.BlockSpec((tb, w), lambda i: (i, 0))
    gate_spec = pl.BlockSpec((None,) + w_gate.shape[1:], lambda i: (layer, 0, 0), pipeline_mode=pl.Buffered(1))
    return pl.pallas_call(
        _merge_kernel,
        grid=(n // tb,),
        in_specs=[tok(d), tok(BR_WIDTH), tok(BR_WIDTH), tok(BR_WIDTH), tok(BR_WIDTH), _const_spec((1, d)),
                  gate_spec, _resident(w_br.shape), _resident(w_out.shape)],
        out_specs=tok(d),
        out_shape=jax.ShapeDtypeStruct((n, d), F32),
        compiler_params=_cparams(("arbitrary",)),
        name="merge",
    )(x, o_gm, o_pool, o_fox, o_sb, g_mix, w_gate, w_br, w_out)


CONV_HALO = 8


def _gelu_tanh(x):
    return 0.5 * x * (1.0 + jnp.tanh(0.7978845608028654 * (x + 0.044715 * (x * x * x))))


def _ffn_kernel(*refs, tb, seg, nblk_seq, final, with_state, row_chunks):
    (x_ref, pe_ref, gf_ref, wug_ref, wuv_ref, cwg_ref, cwv_ref, cbg_ref, cbv_ref, wd_ref,
     gp_ref, wpg_ref, wpp_ref, gfin_ref) = refs[:14]
    refs = refs[14:]
    if with_state:
        p1g_ref, p1v_ref, p2g_ref, p2v_ref = refs[:4]
        refs = refs[4:]
    o_ref, sg_ref, sv_ref, h_ref, acc_ref, cg_ref, cv_ref = refs
    i = pl.program_id(0)
    f = pl.program_id(1)
    nf = pl.num_programs(1)

    @pl.when(f == 0)
    def _():
        h_ref[...] = _rms(x_ref[...], gf_ref[...]).astype(BF)
        acc_ref[...] = jnp.zeros_like(acc_ref)

    @pl.when(i % nblk_seq == 0)
    def _():
        cg_ref[f] = jnp.zeros((CONV_HALO, cg_ref.shape[2]), F32)
        cv_ref[f] = jnp.zeros((CONV_HALO, cv_ref.shape[2]), F32)

    tf = wug_ref.shape[1]
    rc = tb // row_chunks
    tmod = lax.broadcasted_iota(jnp.int32, (rc, tf), 0) % seg
    head_row = lax.broadcasted_iota(jnp.int32, (CONV_HALO, tf), 0)

    def shifted(a, carry, k):
        moved = pltpu.roll(a, k, 0)
        first = jnp.where(head_row < k, pltpu.roll(carry, k, 0), moved[0:CONV_HALO])
        return jnp.concatenate([first, moved[CONV_HALO:]], axis=0) if rc > CONV_HALO else first

    def conv(a, carry, cw_ref, cb_ref, state, rows):
        prev1 = shifted(a, carry, 1)
        prev2 = shifted(a, carry, 2)
        if with_state:
            prev1 = jnp.where(tmod < 1, state[0][rows, :], prev1)
            prev2 = jnp.where(tmod < 2, state[1][rows, :], prev2)
        cw = cw_ref[...]
        return cb_ref[...] + prev2 * cw[0:1, :] + prev1 * cw[1:2, :] + a * cw[2:3, :]

    carry_g, carry_v = cg_ref[f], cv_ref[f]
    for c in range(row_chunks):
        rows = slice(c * rc, (c + 1) * rc)
        hc = h_ref[rows, :]
        ag = _dot(hc, wug_ref[...])
        av = _dot(hc, wuv_ref[...])
        cgate = conv(ag, carry_g, cwg_ref, cbg_ref, (p1g_ref, p2g_ref) if with_state else None, rows)
        cval = conv(av, carry_v, cwv_ref, cbv_ref, (p1v_ref, p2v_ref) if with_state else None, rows)
        carry_g, carry_v = ag[rc - CONV_HALO:rc, :], av[rc - CONV_HALO:rc, :]
        if with_state:
            sg_ref[rows, :] = ag
            sv_ref[rows, :] = av
        acc_ref[rows, :] += _dot((_gelu_tanh(cgate) * cval).astype(BF), wd_ref[...])
    cg_ref[f] = carry_g
    cv_ref[f] = carry_v
    if not with_state:
        sg_ref[...] = carry_g[CONV_HALO - (CONV_W - 1):CONV_HALO, :]
        sv_ref[...] = carry_v[CONV_HALO - (CONV_W - 1):CONV_HALO, :]

    @pl.when(f == nf - 1)
    def _():
        x2 = x_ref[...] + acc_ref[...]
        gate = jax.nn.sigmoid(_dot(_rms(x2, gp_ref[...]).astype(BF), wpg_ref[...]))
        x3 = x2 + gate * _dot(pe_ref[...].astype(BF), wpp_ref[...])
        if final:
            x3 = _rms(x3, gfin_ref[...])
        o_ref[...] = x3


def _ffn(x, pe, g_ffn, w_up, conv_w, conv_b, w_down, g_ple, w_pg, w_pp, g_final, state, *, tb, seg, tf, final):
    n, d = x.shape
    dff = w_down.shape[0]
    nf = dff // tf
    with_state = state is not None
    nblk = n // tb
    nblk_seq = max(1, seg // tb)
    row_chunks = 1
    kern = functools.partial(_ffn_kernel, tb=tb, seg=seg, nblk_seq=nblk_seq, final=final, with_state=with_state,
                             row_chunks=row_chunks)
    tok = lambda w: pl.BlockSpec((tb, w), lambda i, f: (i, 0))
    cst = lambda shape: pl.BlockSpec(shape, lambda i, f: (0,) * len(shape))
    in_specs = [tok(d), tok(pe.shape[1]), cst((1, d)),
                pl.BlockSpec((d, tf), lambda i, f: (0, f)), pl.BlockSpec((d, tf), lambda i, f: (0, nf + f)),
                pl.BlockSpec((CONV_W, tf), lambda i, f: (0, f)), pl.BlockSpec((CONV_W, tf), lambda i, f: (0, nf + f)),
                pl.BlockSpec((1, tf), lambda i, f: (0, f)), pl.BlockSpec((1, tf), lambda i, f: (0, nf + f)),
                pl.BlockSpec((tf, d), lambda i, f: (f, 0)),
                cst((1, d)), _resident(w_pg.shape), _resident(w_pp.shape), cst((1, d))]
    args = [x, pe, g_ffn, w_up, w_up, conv_w, conv_w, conv_b, conv_b, w_down, g_ple, w_pg, w_pp, g_final]
    if with_state:
        p1, p2 = state
        in_specs += [pl.BlockSpec((tb, tf), lambda i, f: (i, f)), pl.BlockSpec((tb, tf), lambda i, f: (i, nf + f)),
                     pl.BlockSpec((tb, tf), lambda i, f: (i, f)), pl.BlockSpec((tb, tf), lambda i, f: (i, nf + f))]
        args += [p1, p1, p2, p2]
        st_shape = jax.ShapeDtypeStruct((n, dff), F32)
        st_spec = pl.BlockSpec((tb, tf), lambda i, f: (i, f))
    else:
        st_shape = jax.ShapeDtypeStruct((nblk, CONV_W - 1, dff), F32)
        st_spec = pl.BlockSpec((None, CONV_W - 1, tf), lambda i, f: (i, 0, f))
    return pl.pallas_call(
        kern,
        grid=(nblk, nf),
        in_specs=in_specs,
        out_specs=[tok(d), st_spec, st_spec],
        out_shape=[jax.ShapeDtypeStruct((n, d), F32), st_shape, st_shape],
        scratch_shapes=[pltpu.VMEM((tb, d), BF), pltpu.VMEM((tb, d), F32),
                        pltpu.VMEM((nf, CONV_HALO, tf), F32), pltpu.VMEM((nf, CONV_HALO, tf), F32)],
        compiler_params=_cparams(("arbitrary", "arbitrary")),
        name="ffn",
    )(*args)


def _pick_block(n, target):
    b = min(n, target)
    while n % b:
        b //= 2
    return b


def kernel(x_prompt, x_sample, cache_fox_k, cache_fox_v, cache_fox_logf, cache_sb_k, cache_sb_v, state_pool, state_ffn_conv, page_table, p_prompt, p_sample, g_mix, w_in, b_f, gm_ln_g, gm_ln_b, gm_ws, gm_bs, pm_w, pm_scale, w_br, w_out, g_ffn, w_up, conv_w, conv_b, w_down, g_ple, w_ple_gate, w_ple_proj, g_final):
    nbp, t, d = x_prompt.shape
    nbs, ts, _ = x_sample.shape
    depth = w_in.shape[0]
    n_pool, page = cache_fox_k.shape[1], cache_fox_k.shape[2]
    n_pages = page_table.shape[1]
    past = n_pages * page
    dff = w_down.shape[1]
    f2 = 2 * dff
    off_mix = N_BRANCH * d
    off_f = off_mix + 6 * BR_WIDTH
    off_sbq = off_f + N_HEADS
    np_tok, ns_tok = nbp * t, nbs * ts

    tb_p = _pick_block(t, 512)
    tb_ffn = _pick_block(t, 512)
    tq = _pick_block(t, 256)
    tq_sb = _pick_block(t, 128)
    tf = dff // 2 if (dff // 2) % LANES == 0 else 256
    npp = _pick_block(n_pages, 8)

    row = lambda a: a.reshape(1, -1)
    xp = x_prompt.reshape(np_tok, d)
    xs = x_sample.reshape(ns_tok, d)
    kv_pages = lambda c: jnp.transpose(c, (0, 1, 3, 4, 2)).reshape(depth, n_pool, BR_WIDTH, page)
    ck_f, cv_f, ck_s, cv_s = kv_pages(cache_fox_k), kv_pages(cache_fox_v), kv_pages(cache_sb_k), kv_pages(cache_sb_v)
    clf = jnp.pad(jnp.swapaxes(cache_fox_logf, 2, 3), ((0, 0), (0, 0), (0, SUBLANES - N_HEADS), (0, 0)))

    gm_tile = lambda a: jnp.tile(a[:, :ts, :ts], (1, nbs, nbs))

    w_t = jnp.transpose(w_in, (2, 0, 1))
    blocks = lambda start, n: [start + W_BLOCK * j for j in range(n // W_BLOCK)]
    mix_starts = (blocks(off_mix, 3 * BR_WIDTH) + blocks(off_mix + 3 * BR_WIDTH, BR_WIDTH) + blocks(off_sbq, BR_WIDTH)
                  + [off_f]
                  + blocks(off_mix + 4 * BR_WIDTH, 2 * BR_WIDTH) + blocks(off_sbq + BR_WIDTH, 2 * BR_WIDTH))
    w_gate_all = _wprep(w_t, blocks(0, off_mix))
    w_mix_all = _wprep(w_t, mix_starts)

    new_p = [[] for _ in range(7)]
    new_s = [[] for _ in range(8)]
    for i in range(depth):
        bf_row = jnp.pad(b_f[i], (0, LANES - N_HEADS)).reshape(1, LANES)
        w_br_b = w_br[i].astype(BF)
        w_out_b = w_out[i].astype(BF)
        w_up_b = w_up[i].astype(BF)
        w_down_b = w_down[i].astype(BF)
        w_pg = w_ple_gate[i].astype(BF)
        w_pp = w_ple_proj[i].astype(BF)
        w_pm_bd = jax.scipy.linalg.block_diag(*[pm_w[i, g] for g in range(len(POOL_WINDOWS))]).astype(BF)
        bs_full = lambda rows: jnp.repeat(jnp.tile(gm_bs[i][:, :min(rows, GM_CHUNK)].T, (rows // min(rows, GM_CHUNK), 1)),
                                          HEAD_DIM, axis=1)
        common = (row(g_mix[i]), w_mix_all, bf_row, row(gm_ln_g[i]), row(gm_ln_b[i]))
        final = i == depth - 1
        heads = lambda a: jnp.transpose(a.reshape(a.shape[0], N_HEADS, HEAD_DIM, a.shape[2]), (0, 3, 1, 2))
        logf = lambda a: jnp.transpose(a[:, :N_HEADS], (0, 2, 1))

        gm_rows = min(GM_CHUNK, tb_p)
        (o_gm, pool_in, fq, sq, _, cum_c, lft, cum_r, fk, fv, sk, sv, fkb, fvb, skb, svb) = _inproj(
            xp, *common, gm_ws[i][:, :gm_rows, :gm_rows], bs_full(gm_rows),
            layer=i, tb=tb_p, seg=t, gm_rows=gm_rows, gm_chunk=GM_CHUNK, nseq=nbp)
        o_pool = _pool(pool_in, w_pm_bd, row(pm_scale[i]), tb=tb_p, rows_per_seq=t, offset=0, pos0=0)
        o_fox = _fox(fq, fkb, fvb, cum_c, cum_r, nseq=nbp, t=t, tq=tq)
        o_sb = _sb(sq, skb, svb, nseq=nbp, t=t, tq=tq_sb)
        x1 = _merge(xp, o_gm, o_pool, o_fox, o_sb, row(g_mix[i]), w_gate_all, w_br_b, w_out_b, layer=i, tb=tb_p)
        xp, st_g, st_v = _ffn(x1, p_prompt[i].reshape(np_tok, -1), row(g_ffn[i]), w_up_b, conv_w[i], row(conv_b[i]),
                              w_down_b, row(g_ple[i]), w_pg, w_pp, row(g_final), None,
                              tb=tb_ffn, seg=t, tf=tf, final=final)
        new_p[0].append(heads(fk))
        new_p[1].append(heads(fv))
        new_p[2].append(logf(lft))
        new_p[3].append(heads(sk))
        new_p[4].append(heads(sv))
        new_p[5].append(pool_in.reshape(nbp, t, BR_WIDTH)[:, t - POOL_BUF:])
        last = slice(t // tb_ffn - 1, None, t // tb_ffn)
        new_p[6].append(jnp.concatenate([st_g[last], st_v[last]], axis=-1))

        (o_gm, pool_in, fq, sq, vn, _, lft, _, fk, fv, sk, sv, _, _, _, _) = _inproj(
            xs, *common, gm_tile(gm_ws[i]), bs_full_sample(gm_bs[i], ts, nbs),
            layer=i, tb=ns_tok, seg=ts, gm_rows=ns_tok, gm_chunk=ts, nseq=1)
        grp_rows = -(-(POOL_BUF + ts) // SUBLANES) * SUBLANES
        lead = grp_rows - POOL_BUF - ts
        full = jnp.concatenate([jnp.zeros((nbs, lead, BR_WIDTH), F32), state_pool[i],
                                pool_in.reshape(nbs, ts, BR_WIDTH)], axis=1)
        o_pool = _pool(full.reshape(nbs * grp_rows, BR_WIDTH), w_pm_bd, row(pm_scale[i]),
                       tb=nbs * grp_rows, rows_per_seq=grp_rows, offset=grp_rows - ts, pos0=past)
        o_pool = o_pool.reshape(nbs, grp_rows, BR_WIDTH)[:, grp_rows - ts:].reshape(ns_tok, BR_WIDTH)
        o_fox, o_sb = _dec_attn(
            page_table, _block_diag_queries(fq, nbs, ts), _block_diag_queries(sq, nbs, ts),
            _pad_new(fk, nbs, ts), _pad_new(fv, nbs, ts),
            jnp.pad(jnp.swapaxes(lft.reshape(SUBLANES, nbs, ts), 0, 1), ((0, 0), (0, 0), (0, LANES - ts))),
            _pad_new(sk, nbs, ts), _pad_new(sv, nbs, ts),
            ck_f, cv_f, clf, ck_s, cv_s, layer=i, npp=npp, tnew=ts)
        o_fox = o_fox.reshape(ns_tok, BR_WIDTH).astype(BF)
        o_sb = o_sb.reshape(ns_tok, BR_WIDTH).astype(BF)
        x1 = _merge(xs, o_gm, o_pool, o_fox, o_sb, row(g_mix[i]), w_gate_all, w_br_b, w_out_b, layer=i, tb=ns_tok)
        buf = state_ffn_conv[i]
        zero = jnp.zeros((nbs, ts - 1, f2), F32)
        p1 = jnp.concatenate([buf[:, 1:2], zero], axis=1).reshape(ns_tok, f2)
        p2 = jnp.concatenate([buf, zero[:, 1:]], axis=1).reshape(ns_tok, f2)
        xs, a_g, a_v = _ffn(x1, p_sample[i].reshape(ns_tok, -1), row(g_ffn[i]), w_up_b, conv_w[i], row(conv_b[i]),
                            w_down_b, row(g_ple[i]), w_pg, w_pp, row(g_final), (p1, p2),
                            tb=ns_tok, seg=ts, tf=tf, final=final)
        new_s[0].append(heads(fk).reshape(nbs, ts, N_HEADS, HEAD_DIM))
        new_s[1].append(heads(fv).reshape(nbs, ts, N_HEADS, HEAD_DIM))
        new_s[2].append(logf(lft).reshape(nbs, ts, N_HEADS))
        new_s[3].append(heads(sk).reshape(nbs, ts, N_HEADS, HEAD_DIM))
        new_s[4].append(heads(sv).reshape(nbs, ts, N_HEADS, HEAD_DIM))
        new_s[5].append(jnp.concatenate([state_pool[i], pool_in.reshape(nbs, ts, BR_WIDTH)], axis=1)[:, ts:])
        a_full = jnp.concatenate([a_g, a_v], axis=-1).reshape(nbs, ts, f2)
        new_s[6].append(jnp.concatenate([buf, a_full], axis=1)[:, ts:])
        new_s[7].append(vn.reshape(nbs, ts, BR_WIDTH))

    stk = lambda lst: jnp.stack(lst, axis=0)
    return (xp.reshape(nbp, t, d), xs.reshape(nbs, ts, d),
            *[stk(a) for a in new_p], *[stk(a) for a in new_s])


def bs_full_sample(gm_bs_i, ts, nbs):
    return jnp.repeat(jnp.tile(gm_bs_i[:, :ts].T, (nbs, 1)), HEAD_DIM, axis=1)


def _block_diag_queries(q, nbs, ts):
    q = q.reshape(nbs, ts, 1, N_HEADS, HEAD_DIM)
    eye = jnp.eye(SUBLANES, N_HEADS, dtype=q.dtype).reshape(1, 1, SUBLANES, N_HEADS, 1)
    return (q * eye).reshape(nbs, ts * SUBLANES, BR_WIDTH)


def _pad_new(a, nbs, ts):
    return jnp.pad(jnp.swapaxes(a.reshape(BR_WIDTH, nbs, ts), 0, 1), ((0, 0), (0, 0), (0, LANES - ts)))
```

```python
import functools

import jax
import jax.numpy as jnp
from jax import lax
from jax.experimental import pallas as pl
from jax.experimental.pallas import tpu as pltpu

F32 = jnp.float32
BF = jnp.bfloat16

EPS = 1e-6
N_BRANCH = 4
BR_WIDTH = 256
N_HEADS = 4
HEAD_DIM = 64
LANES = 128
SUBLANES = 8
POOL_WINDOWS = (2, 4, 8, 16)
POOL_BUF = 15
POOL_HALO = 16
CONV_W = 3
GM_CHUNK = 128
NEG_BIG = -1e30
EXP_ZERO_BELOW = -104.0
VMEM_LIMIT = 48 * 1024 * 1024

C_U, C_V, C_POOL, C_FQ, C_SQ, C_F = [BR_WIDTH * i for i in range(6)]
N_TOKMAJ = C_F + LANES
R_FK, R_FV, R_SK, R_SV = [BR_WIDTH * i for i in range(4)]
N_HEADMAJ = 4 * BR_WIDTH
W_BLOCK = 128
HEAD_GROUP = 4


def _cparams(sem):
    return pltpu.CompilerParams(dimension_semantics=sem, vmem_limit_bytes=VMEM_LIMIT)


def _rms(x, g):
    return x * lax.rsqrt(jnp.mean(x * x, axis=-1, keepdims=True) + EPS) * g


def _softplus(x):
    return jnp.maximum(x, 0.0) + jnp.log1p(jnp.exp(-jnp.abs(x)))


def _split3(x):
    hi = x.astype(BF)
    r = x - hi.astype(F32)
    mid = r.astype(BF)
    r = r - mid.astype(F32)
    return hi, mid, r.astype(BF)


def _dot(a, b):
    return jnp.dot(a, b, preferred_element_type=F32)


def _dot_nt(a, b):
    return lax.dot_general(a, b, (((1,), (1,)), ((), ())), preferred_element_type=F32)


def _dot_exact(x, ones_bf, left, terms=3):
    parts = _split3(x)[:terms]
    out = None
    for p in parts:
        d = _dot(ones_bf, p) if left else _dot(p, ones_bf)
        out = d if out is None else out + d
    return out


def _const_spec(shape):
    nd = len(shape)
    return pl.BlockSpec(shape, lambda *_: (0,) * nd)


def _wprep_kernel(st_ref, w_ref, o_ref):
    del st_ref
    for layer in range(o_ref.shape[0]):
        o_ref[layer] = w_ref[:, layer, :].astype(BF)


def _wprep(w_t, starts):
    n_out, depth, d = w_t.shape
    nblk = len(starts)
    grid_spec = pltpu.PrefetchScalarGridSpec(
        num_scalar_prefetch=1,
        grid=(nblk,),
        in_specs=[pl.BlockSpec((pl.Element(W_BLOCK), pl.Element(depth), pl.Element(d)), lambda j, st: (st[j], 0, 0))],
        out_specs=pl.BlockSpec((depth, W_BLOCK, d), lambda j, st: (0, j, 0)),
    )
    return pl.pallas_call(
        _wprep_kernel,
        grid_spec=grid_spec,
        out_shape=jax.ShapeDtypeStruct((depth, nblk * W_BLOCK, d), BF),
        compiler_params=_cparams(("arbitrary",)),
        name="wprep",
    )(jnp.asarray(starts, jnp.int32), w_t)


def _inproj_kernel(x_ref, g_ref, w_ref, bfr_ref, lng_ref, lnb_ref, ws_ref, bs_ref,
                   ogm_ref, pool_ref, fq_ref, sq_ref, vn_ref, cc_ref, lft_ref, cr_ref,
                   fk_ref, fv_ref, sk_ref, sv_ref, fkb_ref, fvb_ref, skb_ref, svb_ref,
                   carc_ref, *, tb, seg, gm_rows, gm_chunk):
    i = pl.program_id(0)
    h = _rms(x_ref[...], g_ref[...]).astype(BF)
    z = _dot_nt(h, w_ref[0:N_TOKMAJ, :])
    zt = _dot_nt(w_ref[N_TOKMAJ:N_TOKMAJ + N_HEADMAJ, :], h)

    r = lax.broadcasted_iota(jnp.int32, (gm_rows, gm_rows), 0)
    c = lax.broadcasted_iota(jnp.int32, (gm_rows, gm_rows), 1)
    mix_mask = (r >= c) & ((r // gm_chunk) == (c // gm_chunk))
    grp = lax.broadcasted_iota(jnp.int32, (gm_rows, BR_WIDTH), 1) // HEAD_DIM
    for blk in range(tb // gm_rows):
        rows = slice(blk * gm_rows, (blk + 1) * gm_rows)
        u = z[rows, C_U:C_U + BR_WIDTH]
        v = z[rows, C_V:C_V + BR_WIDTH]
        mu = jnp.mean(v, axis=-1, keepdims=True)
        var = jnp.mean(jnp.square(v - mu), axis=-1, keepdims=True)
        vn = (v - mu) * lax.rsqrt(var + EPS) * lng_ref[...] + lnb_ref[...]
        vn_ref[rows, :] = vn
        vnb = vn.astype(BF)
        s = bs_ref[...]
        for g in range(N_HEADS):
            wg = jnp.where(mix_mask, ws_ref[g], 0.0).astype(BF)
            s = s + jnp.where(grp == g, _dot(wg, vnb), 0.0)
        ogm_ref[rows, :] = (u * s).astype(BF)

    pool_ref[...] = z[:, C_POOL:C_POOL + BR_WIDTH]
    scale = HEAD_DIM ** -0.5
    fq_ref[...] = (z[:, C_FQ:C_FQ + BR_WIDTH] * scale).astype(BF)
    sq_ref[...] = (z[:, C_SQ:C_SQ + BR_WIDTH] * scale).astype(BF)
    for start, full_ref, half_ref in ((R_FK, fk_ref, fkb_ref), (R_FV, fv_ref, fvb_ref),
                                      (R_SK, sk_ref, skb_ref), (R_SV, sv_ref, svb_ref)):
        part = zt[start:start + BR_WIDTH, :]
        full_ref[...] = part
        half_ref[...] = part.astype(BF)

    lf = -_softplus(-(z[:, C_F:C_F + LANES] + bfr_ref[...]))
    lft_ref[...] = lf.T[0:SUBLANES, :]
    r = lax.broadcasted_iota(jnp.int32, (tb, tb), 0)
    c = lax.broadcasted_iota(jnp.int32, (tb, tb), 1)
    if seg >= tb:
        lower = (r >= c)
        nblk_seq = seg // tb

        @pl.when(i % nblk_seq == 0)
        def _():
            carc_ref[...] = jnp.zeros_like(carc_ref)
    else:
        lower = (r >= c) & ((r // seg) == (c // seg))
        carc_ref[...] = jnp.zeros_like(carc_ref)
    cum_c = _dot_exact(lf, jnp.where(lower, 1.0, 0.0).astype(BF), left=True) + carc_ref[0:1, :]
    cc_ref[...] = cum_c
    cr_ref[...] = cum_c.T[0:SUBLANES, :]
    carc_ref[...] = jnp.broadcast_to(cum_c[tb - 1:tb, :], carc_ref.shape)


def _inproj(x, g_mix, w_mix, bf_row, ln_g, ln_b, ws, bs, *, layer, tb, seg, gm_rows, gm_chunk, nseq):
    n, d = x.shape
    kern = functools.partial(_inproj_kernel, tb=tb, seg=seg, gm_rows=gm_rows, gm_chunk=gm_chunk)
    tok = lambda w: pl.BlockSpec((tb, w), lambda i: (i, 0))
    per_seq = n // nseq
    nblk_seq = per_seq // tb
    chan = lambda rows: pl.BlockSpec((None, rows, tb), lambda i: (i // nblk_seq, 0, i % nblk_seq))
    tokmaj = lambda width, dt: jax.ShapeDtypeStruct((n, width), dt)
    chanmaj = lambda rows, dt: jax.ShapeDtypeStruct((nseq, rows, per_seq), dt)
    out_shape = [
        tokmaj(BR_WIDTH, BF),
        tokmaj(BR_WIDTH, F32),
        tokmaj(BR_WIDTH, BF),
        tokmaj(BR_WIDTH, BF),
        tokmaj(BR_WIDTH, F32),
        tokmaj(LANES, F32),
        chanmaj(SUBLANES, F32),
        chanmaj(SUBLANES, F32),
    ] + [chanmaj(BR_WIDTH, F32)] * 4 + [chanmaj(BR_WIDTH, BF)] * 4
    out_specs = ([tok(BR_WIDTH)] * 5 + [tok(LANES), chan(SUBLANES), chan(SUBLANES)] + [chan(BR_WIDTH)] * 8)
    w_rows = w_mix.shape[1]
    return pl.pallas_call(
        kern,
        grid=(n // tb,),
        in_specs=[tok(d), _const_spec((1, d)),
                  pl.BlockSpec((None, w_rows, d), lambda i: (layer, 0, 0), pipeline_mode=pl.Buffered(1)),
                  _const_spec(bf_row.shape), _const_spec(ln_g.shape),
                  _const_spec(ln_b.shape), _const_spec(ws.shape), _const_spec(bs.shape)],
        out_specs=out_specs,
        out_shape=out_shape,
        scratch_shapes=[pltpu.VMEM((SUBLANES, LANES), F32)],
        compiler_params=_cparams(("arbitrary",)),
        name="inproj",
    )(x, g_mix, w_mix, bf_row, ln_g, ln_b, ws, bs)


def _pool_kernel(x_ref, w_ref, sc_ref, o_ref, ext_ref, *, tb, rows_per_seq, offset, pos0, nblk_seq):
    i = pl.program_id(0)

    @pl.when(i % nblk_seq == 0)
    def _():
        ext_ref[0:POOL_HALO, :] = jnp.zeros((POOL_HALO, BR_WIDTH), F32)

    ext_ref[POOL_HALO:POOL_HALO + tb, :] = x_ref[...]
    rowi = lax.broadcasted_iota(jnp.int32, (tb, LANES), 0)
    lane = lax.broadcasted_iota(jnp.int32, (tb, LANES), 1)
    pos = pos0 + (i * tb + rowi) % rows_per_seq - offset
    first = lane < HEAD_DIM
    halves = []
    for half in range(2):
        cols = slice(half * LANES, (half + 1) * LANES)
        shifted = lambda j: ext_ref[POOL_HALO - j:POOL_HALO - j + tb, cols]
        w_small, w_big = POOL_WINDOWS[2 * half], POOL_WINDOWS[2 * half + 1]
        cur = shifted(0)
        acc = cur
        for j in range(1, w_small):
            acc = acc + shifted(j)
        small = acc
        for j in range(w_small, w_big):
            acc = acc + shifted(j)
        win = jnp.where(first, small, acc)
        width = jnp.where(first, w_small, w_big)
        cnt = jnp.clip(pos + 1, 1, width).astype(F32)
        halves.append(win / cnt - cur)
    d = jnp.concatenate(halves, axis=1).astype(BF)
    o_ref[...] = (_dot(d, w_ref[...]) * sc_ref[...]).astype(BF)
    ext_ref[0:POOL_HALO, :] = ext_ref[tb:tb + POOL_HALO, :]


def _pool(x, w_bd, scale, *, tb, rows_per_seq, offset, pos0):
    n = x.shape[0]
    kern = functools.partial(_pool_kernel, tb=tb, rows_per_seq=rows_per_seq, offset=offset, pos0=pos0,
                             nblk_seq=max(1, rows_per_seq // tb))
    return pl.pallas_call(
        kern,
        grid=(n // tb,),
        in_specs=[pl.BlockSpec((tb, BR_WIDTH), lambda i: (i, 0)), _const_spec(w_bd.shape), _const_spec(scale.shape)],
        out_specs=pl.BlockSpec((tb, BR_WIDTH), lambda i: (i, 0)),
        out_shape=jax.ShapeDtypeStruct((n, BR_WIDTH), BF),
        scratch_shapes=[pltpu.VMEM((tb + POOL_HALO, BR_WIDTH), F32)],
        compiler_params=_cparams(("arbitrary",)),
        name="pool",
    )(x, w_bd, scale)


def _head_queries(q):
    left = lax.broadcasted_iota(jnp.int32, (q.shape[0], LANES), 1) < HEAD_DIM
    out = []
    for h in range(N_HEADS):
        pair = q[:, (h // 2) * LANES:(h // 2 + 1) * LANES]
        keep = left if h % 2 == 0 else jnp.logical_not(left)
        out.append(jnp.where(keep, pair, jnp.zeros_like(pair)))
    return out, left


def _fox_kernel(q_ref, k_ref, v_ref, cc_ref, cr_ref, o_ref, kn_ref, *, tq):
    qb = pl.program_id(1)
    qh, left = _head_queries(q_ref[...])
    cc = cc_ref[...]
    cq = [cc[:, h:h + 1] for h in range(N_HEADS)]
    r = lax.broadcasted_iota(jnp.int32, (tq, tq), 0)
    c = lax.broadcasted_iota(jnp.int32, (tq, tq), 1)
    causal = c <= r

    @pl.when(qb == 0)
    def _():
        hr = lax.broadcasted_iota(jnp.int32, (SUBLANES, BR_WIDTH), 0)
        hc = lax.broadcasted_iota(jnp.int32, (SUBLANES, BR_WIDTH), 1) // HEAD_DIM
        head_rows = jnp.where(hr == hc, 1.0, 0.0).astype(BF)

        def chunk(ci, best):
            kc = k_ref[:, pl.ds(pl.multiple_of(ci * tq, tq), tq)].astype(F32)
            n2 = _dot_exact(kc * kc, head_rows, left=True)
            return jnp.maximum(best, jnp.max(n2, axis=1, keepdims=True))

        best = lax.fori_loop(0, k_ref.shape[1] // tq, chunk, jnp.zeros((SUBLANES, 1), F32))
        kn_ref[...] = jnp.broadcast_to(best, kn_ref.shape)

    reach = []
    for h in range(N_HEADS):
        qf = qh[h].astype(F32)
        qn2 = jnp.sum(qf * qf, axis=-1, keepdims=True)
        reach.append(jnp.sqrt(qn2 * kn_ref[h:h + 1, 0:1]) * 1.001 + 0.01 + cq[h])

    def block(j, carry, masked):
        ms, ls, accs = carry
        start = pl.multiple_of(j * tq, tq)
        kblk = k_ref[:, pl.ds(start, tq)]
        vblk = v_ref[:, pl.ds(start, tq)]
        pairs = [slice((h // 2) * LANES, (h // 2 + 1) * LANES) for h in range(N_HEADS)]
        new_m, new_l, alphas, pv = ([None] * N_HEADS for _ in range(4))
        for g in range(0, N_HEADS, HEAD_GROUP):
            heads = range(g, g + HEAD_GROUP)
            ss = {h: _dot(qh[h], kblk[pairs[h], :]) + (cq[h] - cr_ref[h:h + 1, pl.ds(start, tq)]) for h in heads}
            if masked:
                ss = {h: jnp.where(causal, ss[h], NEG_BIG) for h in heads}
            for h in heads:
                new_m[h] = jnp.maximum(ms[h], jnp.max(ss[h], axis=-1, keepdims=True))
            for h in heads:
                alphas[h] = jnp.exp(ms[h] - new_m[h])
            ps = {h: jnp.exp(ss[h] - new_m[h]) for h in heads}
            for h in heads:
                new_l[h] = ls[h] * alphas[h] + jnp.sum(ps[h], axis=-1, keepdims=True)
            for h in heads:
                pv[h] = _dot_nt(ps[h].astype(BF), vblk[pairs[h], :])
        new_acc = []
        for hp in range(N_HEADS // 2):
            a = jnp.where(left, alphas[2 * hp], alphas[2 * hp + 1])
            new_acc.append(accs[hp] * a + jnp.where(left, pv[2 * hp], pv[2 * hp + 1]))
        return tuple(new_m), tuple(new_l), tuple(new_acc)

    init = (tuple(jnp.full((tq, 1), NEG_BIG, F32) for _ in range(N_HEADS)),
            tuple(jnp.zeros((tq, 1), F32) for _ in range(N_HEADS)),
            tuple(jnp.zeros((tq, LANES), F32) for _ in range(N_HEADS // 2)))
    def live(state):
        j, ms = state[0], state[1]
        start = pl.multiple_of(jnp.maximum(j, 0) * tq, tq)
        gap = None
        for h in range(N_HEADS):
            ck_last = cr_ref[h:h + 1, pl.ds(start, tq)][:, tq - 1:tq]
            g = reach[h] - ck_last - ms[h]
            gap = g if gap is None else jnp.maximum(gap, g)
        return (j >= 0) & (jnp.max(gap) > EXP_ZERO_BELOW)

    def step(state):
        j = state[0]
        return (j - 1,) + block(j, state[1:], False)

    first = block(qb, init, True)
    _, ms, ls, accs = lax.while_loop(live, step, (qb - 1,) + first)
    for hp in range(N_HEADS // 2):
        den = jnp.where(left, ls[2 * hp], ls[2 * hp + 1])
        o_ref[:, hp * LANES:(hp + 1) * LANES] = (accs[hp] / den).astype(BF)


def _fox(q, kb, vb, cum_c, cum_r, *, nseq, t, tq):
    kern = functools.partial(_fox_kernel, tq=tq)
    nq = t // tq
    return pl.pallas_call(
        kern,
        grid=(nseq, nq),
        in_specs=[pl.BlockSpec((tq, BR_WIDTH), lambda n, i: (n * nq + i, 0)),
                  pl.BlockSpec((None, BR_WIDTH, t), lambda n, i: (n, 0, 0)),
                  pl.BlockSpec((None, BR_WIDTH, t), lambda n, i: (n, 0, 0)),
                  pl.BlockSpec((tq, LANES), lambda n, i: (n * nq + i, 0)),
                  pl.BlockSpec((None, SUBLANES, t), lambda n, i: (n, 0, 0))],
        out_specs=pl.BlockSpec((tq, BR_WIDTH), lambda n, i: (n * nq + i, 0)),
        out_shape=jax.ShapeDtypeStruct((nseq * t, BR_WIDTH), BF),
        scratch_shapes=[pltpu.VMEM((SUBLANES, LANES), F32)],
        compiler_params=_cparams(("arbitrary", "arbitrary")),
        name="fox_prompt",
    )(q, kb, vb, cum_c, cum_r)


def _sb_kernel(q_ref, k_ref, v_ref, o_ref, *, tq):
    qb = pl.program_id(1)
    qh, left = _head_queries(q_ref[...])
    r = lax.broadcasted_iota(jnp.int32, (tq, tq), 0)
    c = lax.broadcasted_iota(jnp.int32, (tq, tq), 1)
    strict = c < r
    after = jnp.where(r > c, 1.0, 0.0).astype(BF)

    def block(j, carry, masked):
        cs, accs = carry
        start = pl.multiple_of(j * tq, tq)
        kblk = k_ref[:, pl.ds(start, tq)]
        vblk = v_ref[:, pl.ds(start, tq)]
        pairs = [slice((h // 2) * LANES, (h // 2 + 1) * LANES) for h in range(N_HEADS)]
        zs = [_dot(qh[h], kblk[pairs[h], :]) for h in range(N_HEADS)]
        lrests = [-_softplus(z) for z in zs]
        if masked:
            lrests = [jnp.where(strict, l, 0.0) for l in lrests]
        laters = [_dot_exact(lrests[h], after, left=False, terms=2) + cs[h] for h in range(N_HEADS)]
        weights = [jnp.exp(zs[h] + lrests[h] + laters[h]) for h in range(N_HEADS)]
        if masked:
            weights = [jnp.where(strict, a, 0.0) for a in weights]
        av = [_dot_nt(weights[h].astype(BF), vblk[pairs[h], :]) for h in range(N_HEADS)]
        new_c = [cs[h] + jnp.sum(lrests[h], axis=-1, keepdims=True) for h in range(N_HEADS)]
        new_acc = tuple(accs[hp] + jnp.where(left, av[2 * hp], av[2 * hp + 1]) for hp in range(N_HEADS // 2))
        return tuple(new_c), new_acc

    init = (tuple(jnp.zeros((tq, 1), F32) for _ in range(N_HEADS)),
            tuple(jnp.zeros((tq, LANES), F32) for _ in range(N_HEADS // 2)))
    def live(state):
        j, cs = state[0], state[1]
        top = cs[0]
        for h in range(1, N_HEADS):
            top = jnp.maximum(top, cs[h])
        return (j >= 0) & (jnp.max(top) > EXP_ZERO_BELOW)

    def step(state):
        j = state[0]
        return (j - 1,) + block(j, state[1:], False)

    first = block(qb, init, True)
    _, _, accs = lax.while_loop(live, step, (qb - 1,) + first)
    for hp in range(N_HEADS // 2):
        o_ref[:, hp * LANES:(hp + 1) * LANES] = accs[hp].astype(BF)


def _sb(q, kb, vb, *, nseq, t, tq):
    kern = functools.partial(_sb_kernel, tq=tq)
    nq = t // tq
    return pl.pallas_call(
        kern,
        grid=(nseq, nq),
        in_specs=[pl.BlockSpec((tq, BR_WIDTH), lambda n, i: (n * nq + i, 0)),
                  pl.BlockSpec((None, BR_WIDTH, t), lambda n, i: (n, 0, 0)),
                  pl.BlockSpec((None, BR_WIDTH, t), lambda n, i: (n, 0, 0))],
        out_specs=pl.BlockSpec((tq, BR_WIDTH), lambda n, i: (n * nq + i, 0)),
        out_shape=jax.ShapeDtypeStruct((nseq * t, BR_WIDTH), BF),
        compiler_params=_cparams(("arbitrary", "arbitrary")),
        name="sb_prompt",
    )(q, kb, vb)


N_CACHES = 5
KF, VF, LF, KS, VS = range(N_CACHES)
K_AHEAD = 2


def _dec_attn_kernel(pt_ref, qf_ref, qs_ref, knf_ref, vnf_ref, lfn_ref, kns_ref, vns_ref,
                     ckf_hbm, cvf_hbm, clf_hbm, cks_hbm, cvs_hbm, of_ref, os_ref,
                     kf_buf, vf_buf, lf_buf, ks_buf, vs_buf, sems,
                     m_ref, l_ref, accf_ref, offf_ref, e_ref, cs_ref, accs_ref,
                     *, layer, npp, n_pages, tnew):
    DEC_ROWS = tnew * SUBLANES
    b_id = pl.program_id(0)
    n_chunks = n_pages // npp
    hbm = (ckf_hbm, cvf_hbm, clf_hbm, cks_hbm, cvs_hbm)
    bufs = (kf_buf, vf_buf, lf_buf, ks_buf, vs_buf)

    def chunk_copies(which, chunk, slot):
        first = n_pages - (chunk + 1) * npp
        return [pltpu.make_async_copy(hbm[which].at[layer, pt_ref[b_id, first + j]],
                                      bufs[which].at[slot, j], sems.at[which, slot]) for j in range(npp)]

    def start(which, chunk, slot):
        for cp in chunk_copies(which, chunk, slot):
            cp.start()

    def wait(which, chunk, slot):
        for cp in chunk_copies(which, chunk, slot):
            cp.wait()

    r = lax.broadcasted_iota(jnp.int32, (LANES, LANES), 0)
    c = lax.broadcasted_iota(jnp.int32, (LANES, LANES), 1)
    after = jnp.where(r > c, 1.0, 0.0).astype(BF)
    rowi = lax.broadcasted_iota(jnp.int32, (DEC_ROWS, LANES), 0)
    lane = lax.broadcasted_iota(jnp.int32, (DEC_ROWS, LANES), 1)
    t_row = rowi // SUBLANES
    real_head = rowi % SUBLANES < N_HEADS

    def suffix(l_list):
        nb = len(l_list)
        stack = l_list[0] if nb == 1 else jnp.concatenate(l_list, axis=0)
        inner = _dot_exact(stack, after, left=False)
        out = [None] * nb
        off = jnp.zeros((DEC_ROWS, 1), F32)
        for b in range(nb - 1, -1, -1):
            out[b] = inner[b * DEC_ROWS:(b + 1) * DEC_ROWS] + off
            off = off + jnp.sum(l_list[b], axis=-1, keepdims=True)
        return out, off

    def fox_logits(kf, lf8, is_new):
        nb = len(kf)
        qf = qf_ref[...]
        lf = [jnp.concatenate([x] * (DEC_ROWS // SUBLANES), axis=0) for x in lf8]
        rsum, tot = suffix(lf)
        base = offf_ref[...] - e_ref[...]
        s_list = []
        for b in range(nb):
            s = _dot(qf, kf[b].astype(BF)) + rsum[b] + base
            if is_new:
                s = jnp.where((lane <= t_row) & (lane < tnew), s, NEG_BIG)
            s_list.append(s)
        offf_ref[...] = offf_ref[...] + tot
        m_blk = s_list[0]
        for b in range(1, nb):
            m_blk = jnp.maximum(m_blk, s_list[b])
        m_new = jnp.maximum(m_ref[...], jnp.max(m_blk, axis=-1, keepdims=True))
        needed = jnp.max(jnp.where(real_head, m_blk - m_new, NEG_BIG)) > EXP_ZERO_BELOW
        return s_list, m_new, needed

    def fox_update(s_list, m_new, vf):
        m_old = m_ref[...]
        alpha = jnp.exp(m_old - m_new)
        psum = jnp.zeros((DEC_ROWS, LANES), F32)
        pv = jnp.zeros((DEC_ROWS, BR_WIDTH), F32)
        for b in range(len(s_list)):
            p = jnp.exp(s_list[b] - m_new)
            psum = psum + p
            pv = pv + _dot_nt(p.astype(BF), vf[b].astype(BF))
        m_ref[...] = m_new
        l_ref[...] = l_ref[...] * alpha + jnp.sum(psum, axis=-1, keepdims=True)
        accf_ref[...] = accf_ref[...] * alpha[:, 0:1] + pv

    def sb_part(ks, vs, is_new):
        nb = len(ks)
        qs = qs_ref[...]
        z_list, l_list = [], []
        for b in range(nb):
            z = _dot(qs, ks[b].astype(BF))
            lrest = -_softplus(z)
            if is_new:
                lrest = jnp.where((lane < t_row) & (lane < tnew), lrest, 0.0)
            z_list.append(z)
            l_list.append(lrest)
        later, tot_s = suffix(l_list)
        cs = cs_ref[...]
        av = jnp.zeros((DEC_ROWS, BR_WIDTH), F32)
        for b in range(nb):
            a = jnp.exp(z_list[b] + l_list[b] + later[b] + cs)
            if is_new:
                a = jnp.where((lane < t_row) & (lane < tnew), a, 0.0)
            av = av + _dot_nt(a.astype(BF), vs[b].astype(BF))
        accs_ref[...] = accs_ref[...] + av
        cs_ref[...] = cs + tot_s

    def sb_alive():
        return jnp.max(jnp.where(real_head, cs_ref[...], NEG_BIG)) > EXP_ZERO_BELOW

    for which in range(N_CACHES):
        start(which, 0, 0)
    for ahead in range(1, min(K_AHEAD, n_chunks)):
        start(KF, ahead, ahead)
        start(LF, ahead, ahead)

    m_ref[...] = jnp.full(m_ref.shape, NEG_BIG, F32)
    l_ref[...] = jnp.zeros_like(l_ref)
    accf_ref[...] = jnp.zeros_like(accf_ref)
    offf_ref[...] = jnp.zeros_like(offf_ref)
    cs_ref[...] = jnp.zeros_like(cs_ref)
    accs_ref[...] = jnp.zeros_like(accs_ref)
    lfn = jnp.concatenate([lfn_ref[...]] * (DEC_ROWS // SUBLANES), axis=0)
    inner = _dot_exact(lfn, after, left=False)
    e = jnp.sum(jnp.where(lane == t_row, inner, 0.0), axis=-1, keepdims=True)
    e_ref[...] = jnp.broadcast_to(e, e_ref.shape)
    s_new, m_new, _ = fox_logits([knf_ref[...]], [lfn_ref[...]], True)
    fox_update(s_new, m_new, [vnf_ref[...]])
    sb_part([kns_ref[...]], [vns_ref[...]], True)

    def chunk_step(c, carry):
        v_here, sb_here = carry
        slot = c % 2
        kslot = c % (K_AHEAD + 1)
        more = c + 1 < n_chunks
        pages_of = lambda which, s=None: [bufs[which][slot if s is None else s, j] for j in range(npp)]

        wait(KF, c, kslot)
        wait(LF, c, kslot)

        @pl.when(c + K_AHEAD < n_chunks)
        def _():
            start(KF, c + K_AHEAD, (c + K_AHEAD) % (K_AHEAD + 1))
            start(LF, c + K_AHEAD, (c + K_AHEAD) % (K_AHEAD + 1))

        s_list, m_new, needed = fox_logits(pages_of(KF, kslot), pages_of(LF, kslot), False)

        @pl.when(needed & (v_here == 0))
        def _():
            start(VF, c, slot)

        @pl.when(needed | (v_here == 1))
        def _():
            wait(VF, c, slot)

        v_next = jnp.where(needed & more, 1, 0)

        @pl.when(v_next == 1)
        def _():
            start(VF, c + 1, 1 - slot)

        @pl.when(needed)
        def _():
            fox_update(s_list, m_new, pages_of(VF))

        @pl.when(sb_here == 1)
        def _():
            wait(KS, c, slot)
            wait(VS, c, slot)
            sb_part(pages_of(KS), pages_of(VS), False)

        sb_next = jnp.where((sb_here == 1) & more & sb_alive(), 1, 0)

        @pl.when(sb_next == 1)
        def _():
            start(KS, c + 1, 1 - slot)
            start(VS, c + 1, 1 - slot)

        return v_next, sb_next

    lax.fori_loop(0, n_chunks, chunk_step, (jnp.int32(1), jnp.int32(1)))

    own = (lax.broadcasted_iota(jnp.int32, (DEC_ROWS, BR_WIDTH), 1) // HEAD_DIM
           == lax.broadcasted_iota(jnp.int32, (DEC_ROWS, BR_WIDTH), 0) % SUBLANES)
    nt = DEC_ROWS // SUBLANES
    fo = jnp.where(own, accf_ref[...] / l_ref[:, 0:1], 0.0).reshape(nt, SUBLANES, BR_WIDTH)
    so = jnp.where(own, accs_ref[...], 0.0).reshape(nt, SUBLANES, BR_WIDTH)
    of_ref[...] = jnp.sum(fo, axis=1)
    os_ref[...] = jnp.sum(so, axis=1)


def _dec_attn(page_table, qf, qs, knf, vnf, lfn, kns, vns, ck_f, cv_f, clf, ck_s, cv_s, *, layer, npp, tnew):
    nb, n_pages = page_table.shape
    page = ck_f.shape[3]
    DEC_ROWS = tnew * SUBLANES
    kern = functools.partial(_dec_attn_kernel, layer=layer, npp=npp, n_pages=n_pages, tnew=tnew)
    per_b = lambda shape: pl.BlockSpec((None,) + shape, lambda b, pt: (b, 0, 0))
    in_hbm = pl.BlockSpec(memory_space=pl.ANY)
    in_specs = [per_b((DEC_ROWS, BR_WIDTH)), per_b((DEC_ROWS, BR_WIDTH)),
                per_b((BR_WIDTH, LANES)), per_b((BR_WIDTH, LANES)), per_b((SUBLANES, LANES)),
                per_b((BR_WIDTH, LANES)), per_b((BR_WIDTH, LANES))] + [in_hbm] * N_CACHES
    nt = DEC_ROWS // SUBLANES
    out_spec = pl.BlockSpec((None, nt, BR_WIDTH), lambda b, pt: (b, 0, 0))
    wide = pltpu.VMEM((DEC_ROWS, BR_WIDTH), F32)
    narrow = pltpu.VMEM((DEC_ROWS, LANES), F32)
    kv_slots = pltpu.VMEM((2, npp, BR_WIDTH, page), F32)
    kf_slots = pltpu.VMEM((K_AHEAD + 1, npp, BR_WIDTH, page), F32)
    lf_slots = pltpu.VMEM((K_AHEAD + 1, npp, SUBLANES, page), F32)
    grid_spec = pltpu.PrefetchScalarGridSpec(
        num_scalar_prefetch=1,
        grid=(nb,),
        in_specs=in_specs,
        out_specs=[out_spec, out_spec],
        scratch_shapes=[kf_slots, kv_slots, lf_slots, kv_slots, kv_slots,
                        pltpu.SemaphoreType.DMA((N_CACHES, K_AHEAD + 1)),
                        narrow, narrow, wide, narrow, narrow, narrow, wide],
    )
    return pl.pallas_call(
        kern,
        grid_spec=grid_spec,
        out_shape=[jax.ShapeDtypeStruct((nb, nt, BR_WIDTH), F32)] * 2,
        compiler_params=_cparams(("arbitrary",)),
        name="dec_attn",
    )(page_table, qf, qs, knf, vnf, lfn, kns, vns, ck_f, cv_f, clf, ck_s, cv_s)


def _merge_kernel(x_ref, og_ref, op_ref, of_ref, os_ref, g_ref, wg_ref, wb_ref, wo_ref, o_ref):
    x = x_ref[...]
    d = x.shape[1]
    h = _rms(x, g_ref[...]).astype(BF)
    acc = jnp.zeros(x.shape, F32)
    for b, ref in enumerate((og_ref, op_ref, of_ref, os_ref)):
        gate = jax.nn.sigmoid(_dot_nt(h, wg_ref[b * d:(b + 1) * d, :]))
        acc = acc + gate * _dot(ref[...], wb_ref[b])
    o_ref[...] = x + _dot(acc.astype(BF), wo_ref[...])


def _resident(shape):
    nd = len(shape)
    return pl.BlockSpec(shape, lambda *_: (0,) * nd, pipeline_mode=pl.Buffered(1))


def _merge(x, o_gm, o_pool, o_fox, o_sb, g_mix, w_gate, w_br, w_out, *, layer, tb):
    n, d = x.shape
    tok = lambda w: pl.BlockSpec((tb, w), lambda i: (i, 0))
    gate_spec = pl.BlockSpec((None,) + w_gate.shape[1:], lambda i: (layer, 0, 0), pipeline_mode=pl.Buffered(1))
    return pl.pallas_call(
        _merge_kernel,
        grid=(n // tb,),
        in_specs=[tok(d), tok(BR_WIDTH), tok(BR_WIDTH), tok(BR_WIDTH), tok(BR_WIDTH), _const_spec((1, d)),
                  gate_spec, _resident(w_br.shape), _resident(w_out.shape)],
        out_specs=tok(d),
        out_shape=jax.ShapeDtypeStruct((n, d), F32),
        compiler_params=_cparams(("arbitrary",)),
        name="merge",
    )(x, o_gm, o_pool, o_fox, o_sb, g_mix, w_gate, w_br, w_out)


CONV_HALO = 8


def _gelu_tanh(x):
    return 0.5 * x * (1.0 + jnp.tanh(0.7978845608028654 * (x + 0.044715 * (x * x * x))))


def _ffn_kernel(*refs, tb, seg, nblk_seq, final, with_state, row_chunks):
    (x_ref, pe_ref, gf_ref, wug_ref, wuv_ref, cwg_ref, cwv_ref, cbg_ref, cbv_ref, wd_ref,
     gp_ref, wpg_ref, wpp_ref, gfin_ref) = refs[:14]
    refs = refs[14:]
    if with_state:
        p1g_ref, p1v_ref, p2g_ref, p2v_ref = refs[:4]
        refs = refs[4:]
    o_ref, sg_ref, sv_ref, h_ref, acc_ref, cg_ref, cv_ref = refs
    i = pl.program_id(0)
    f = pl.program_id(1)
    nf = pl.num_programs(1)

    @pl.when(f == 0)
    def _():
        h_ref[...] = _rms(x_ref[...], gf_ref[...]).astype(BF)
        acc_ref[...] = jnp.zeros_like(acc_ref)

    @pl.when(i % nblk_seq == 0)
    def _():
        cg_ref[f] = jnp.zeros((CONV_HALO, cg_ref.shape[2]), F32)
        cv_ref[f] = jnp.zeros((CONV_HALO, cv_ref.shape[2]), F32)

    tf = wug_ref.shape[1]
    rc = tb // row_chunks
    tmod = lax.broadcasted_iota(jnp.int32, (rc, tf), 0) % seg
    head_row = lax.broadcasted_iota(jnp.int32, (CONV_HALO, tf), 0)

    def shifted(a, carry, k):
        moved = pltpu.roll(a, k, 0)
        first = jnp.where(head_row < k, pltpu.roll(carry, k, 0), moved[0:CONV_HALO])
        return jnp.concatenate([first, moved[CONV_HALO:]], axis=0) if rc > CONV_HALO else first

    def conv(a, carry, cw_ref, cb_ref, state, rows):
        prev1 = shifted(a, carry, 1)
        prev2 = shifted(a, carry, 2)
        if with_state:
            prev1 = jnp.where(tmod < 1, state[0][rows, :], prev1)
            prev2 = jnp.where(tmod < 2, state[1][rows, :], prev2)
        cw = cw_ref[...]
        return cb_ref[...] + prev2 * cw[0:1, :] + prev1 * cw[1:2, :] + a * cw[2:3, :]

    carry_g, carry_v = cg_ref[f], cv_ref[f]
    for c in range(row_chunks):
        rows = slice(c * rc, (c + 1) * rc)
        hc = h_ref[rows, :]
        ag = _dot(hc, wug_ref[...])
        av = _dot(hc, wuv_ref[...])
        cgate = conv(ag, carry_g, cwg_ref, cbg_ref, (p1g_ref, p2g_ref) if with_state else None, rows)
        cval = conv(av, carry_v, cwv_ref, cbv_ref, (p1v_ref, p2v_ref) if with_state else None, rows)
        carry_g, carry_v = ag[rc - CONV_HALO:rc, :], av[rc - CONV_HALO:rc, :]
        if with_state:
            sg_ref[rows, :] = ag
            sv_ref[rows, :] = av
        acc_ref[rows, :] += _dot((_gelu_tanh(cgate) * cval).astype(BF), wd_ref[...])
    cg_ref[f] = carry_g
    cv_ref[f] = carry_v
    if not with_state:
        sg_ref[...] = carry_g[CONV_HALO - (CONV_W - 1):CONV_HALO, :]
        sv_ref[...] = carry_v[CONV_HALO - (CONV_W - 1):CONV_HALO, :]

    @pl.when(f == nf - 1)
    def _():
        x2 = x_ref[...] + acc_ref[...]
        gate = jax.nn.sigmoid(_dot(_rms(x2, gp_ref[...]).astype(BF), wpg_ref[...]))
        x3 = x2 + gate * _dot(pe_ref[...].astype(BF), wpp_ref[...])
        if final:
            x3 = _rms(x3, gfin_ref[...])
        o_ref[...] = x3


def _ffn(x, pe, g_ffn, w_up, conv_w, conv_b, w_down, g_ple, w_pg, w_pp, g_final, state, *, tb, seg, tf, final):
    n, d = x.shape
    dff = w_down.shape[0]
    nf = dff // tf
    with_state = state is not None
    nblk = n // tb
    nblk_seq = max(1, seg // tb)
    row_chunks = 1
    kern = functools.partial(_ffn_kernel, tb=tb, seg=seg, nblk_seq=nblk_seq, final=final, with_state=with_state,
                             row_chunks=row_chunks)
    tok = lambda w: pl.BlockSpec((tb, w), lambda i, f: (i, 0))
    cst = lambda shape: pl.BlockSpec(shape, lambda i, f: (0,) * len(shape))
    in_specs = [tok(d), tok(pe.shape[1]), cst((1, d)),
                pl.BlockSpec((d, tf), lambda i, f: (0, f)), pl.BlockSpec((d, tf), lambda i, f: (0, nf + f)),
                pl.BlockSpec((CONV_W, tf), lambda i, f: (0, f)), pl.BlockSpec((CONV_W, tf), lambda i, f: (0, nf + f)),
                pl.BlockSpec((1, tf), lambda i, f: (0, f)), pl.BlockSpec((1, tf), lambda i, f: (0, nf + f)),
                pl.BlockSpec((tf, d), lambda i, f: (f, 0)),
                cst((1, d)), _resident(w_pg.shape), _resident(w_pp.shape), cst((1, d))]
    args = [x, pe, g_ffn, w_up, w_up, conv_w, conv_w, conv_b, conv_b, w_down, g_ple, w_pg, w_pp, g_final]
    if with_state:
        p1, p2 = state
        in_specs += [pl.BlockSpec((tb, tf), lambda i, f: (i, f)), pl.BlockSpec((tb, tf), lambda i, f: (i, nf + f)),
                     pl.BlockSpec((tb, tf), lambda i, f: (i, f)), pl.BlockSpec((tb, tf), lambda i, f: (i, nf + f))]
        args += [p1, p1, p2, p2]
        st_shape = jax.ShapeDtypeStruct((n, dff), F32)
        st_spec = pl.BlockSpec((tb, tf), lambda i, f: (i, f))
    else:
        st_shape = jax.ShapeDtypeStruct((nblk, CONV_W - 1, dff), F32)
        st_spec = pl.BlockSpec((None, CONV_W - 1, tf), lambda i, f: (i, 0, f))
    return pl.pallas_call(
        kern,
        grid=(nblk, nf),
        in_specs=in_specs,
        out_specs=[tok(d), st_spec, st_spec],
        out_shape=[jax.ShapeDtypeStruct((n, d), F32), st_shape, st_shape],
        scratch_shapes=[pltpu.VMEM((tb, d), BF), pltpu.VMEM((tb, d), F32),
                        pltpu.VMEM((nf, CONV_HALO, tf), F32), pltpu.VMEM((nf, CONV_HALO, tf), F32)],
        compiler_params=_cparams(("arbitrary", "arbitrary")),
        name="ffn",
    )(*args)


def _pick_block(n, target):
    b = min(n, target)
    while n % b:
        b //= 2
    return b


def kernel(x_prompt, x_sample, cache_fox_k, cache_fox_v, cache_fox_logf, cache_sb_k, cache_sb_v, state_pool, state_ffn_conv, page_table, p_prompt, p_sample, g_mix, w_in, b_f, gm_ln_g, gm_ln_b, gm_ws, gm_bs, pm_w, pm_scale, w_br, w_out, g_ffn, w_up, conv_w, conv_b, w_down, g_ple, w_ple_gate, w_ple_proj, g_final):
    nbp, t, d = x_prompt.shape
    nbs, ts, _ = x_sample.shape
    depth = w_in.shape[0]
    n_pool, page = cache_fox_k.shape[1], cache_fox_k.shape[2]
    n_pages = page_table.shape[1]
    past = n_pages * page
    dff = w_down.shape[1]
    f2 = 2 * dff
    off_mix = N_BRANCH * d
    off_f = off_mix + 6 * BR_WIDTH
    off_sbq = off_f + N_HEADS
    np_tok, ns_tok = nbp * t, nbs * ts

    tb_p = _pick_block(t, 512)
    tb_ffn = _pick_block(t, 512)
    tq = _pick_block(t, 256)
    tq_sb = tq
    tf = dff // 2 if (dff // 2) % LANES == 0 else 256
    npp = _pick_block(n_pages, 8)

    row = lambda a: a.reshape(1, -1)
    xp = x_prompt.reshape(np_tok, d)
    xs = x_sample.reshape(ns_tok, d)
    kv_pages = lambda c: jnp.transpose(c, (0, 1, 3, 4, 2)).reshape(depth, n_pool, BR_WIDTH, page)
    ck_f, cv_f, ck_s, cv_s = kv_pages(cache_fox_k), kv_pages(cache_fox_v), kv_pages(cache_sb_k), kv_pages(cache_sb_v)
    clf = jnp.pad(jnp.swapaxes(cache_fox_logf, 2, 3), ((0, 0), (0, 0), (0, SUBLANES - N_HEADS), (0, 0)))

    gm_tile = lambda a: jnp.tile(a[:, :ts, :ts], (1, nbs, nbs))

    w_t = jnp.transpose(w_in, (2, 0, 1))
    blocks = lambda start, n: [start + W_BLOCK * j for j in range(n // W_BLOCK)]
    mix_starts = (blocks(off_mix, 3 * BR_WIDTH) + blocks(off_mix + 3 * BR_WIDTH, BR_WIDTH) + blocks(off_sbq, BR_WIDTH)
                  + [off_f]
                  + blocks(off_mix + 4 * BR_WIDTH, 2 * BR_WIDTH) + blocks(off_sbq + BR_WIDTH, 2 * BR_WIDTH))
    w_gate_all = _wprep(w_t, blocks(0, off_mix))
    w_mix_all = _wprep(w_t, mix_starts)

    new_p = [[] for _ in range(7)]
    new_s = [[] for _ in range(8)]
    for i in range(depth):
        bf_row = jnp.pad(b_f[i], (0, LANES - N_HEADS)).reshape(1, LANES)
        w_br_b = w_br[i].astype(BF)
        w_out_b = w_out[i].astype(BF)
        w_up_b = w_up[i].astype(BF)
        w_down_b = w_down[i].astype(BF)
        w_pg = w_ple_gate[i].astype(BF)
        w_pp = w_ple_proj[i].astype(BF)
        w_pm_bd = jax.scipy.linalg.block_diag(*[pm_w[i, g] for g in range(len(POOL_WINDOWS))]).astype(BF)
        bs_full = lambda rows: jnp.repeat(jnp.tile(gm_bs[i][:, :min(rows, GM_CHUNK)].T, (rows // min(rows, GM_CHUNK), 1)),
                                          HEAD_DIM, axis=1)
        common = (row(g_mix[i]), w_mix_all, bf_row, row(gm_ln_g[i]), row(gm_ln_b[i]))
        final = i == depth - 1
        heads = lambda a: jnp.transpose(a.reshape(a.shape[0], N_HEADS, HEAD_DIM, a.shape[2]), (0, 3, 1, 2))
        logf = lambda a: jnp.transpose(a[:, :N_HEADS], (0, 2, 1))

        gm_rows = min(GM_CHUNK, tb_p)
        (o_gm, pool_in, fq, sq, _, cum_c, lft, cum_r, fk, fv, sk, sv, fkb, fvb, skb, svb) = _inproj(
            xp, *common, gm_ws[i][:, :gm_rows, :gm_rows], bs_full(gm_rows),
            layer=i, tb=tb_p, seg=t, gm_rows=gm_rows, gm_chunk=GM_CHUNK, nseq=nbp)
        o_pool = _pool(pool_in, w_pm_bd, row(pm_scale[i]), tb=tb_p, rows_per_seq=t, offset=0, pos0=0)
        o_fox = _fox(fq, fkb, fvb, cum_c, cum_r, nseq=nbp, t=t, tq=tq)
        o_sb = _sb(sq, skb, svb, nseq=nbp, t=t, tq=tq_sb)
        x1 = _merge(xp, o_gm, o_pool, o_fox, o_sb, row(g_mix[i]), w_gate_all, w_br_b, w_out_b, layer=i, tb=tb_p)
        xp, st_g, st_v = _ffn(x1, p_prompt[i].reshape(np_tok, -1), row(g_ffn[i]), w_up_b, conv_w[i], row(conv_b[i]),
                              w_down_b, row(g_ple[i]), w_pg, w_pp, row(g_final), None,
                              tb=tb_ffn, seg=t, tf=tf, final=final)
        new_p[0].append(heads(fk))
        new_p[1].append(heads(fv))
        new_p[2].append(logf(lft))
        new_p[3].append(heads(sk))
        new_p[4].append(heads(sv))
        new_p[5].append(pool_in.reshape(nbp, t, BR_WIDTH)[:, t - POOL_BUF:])
        last = slice(t // tb_ffn - 1, None, t // tb_ffn)
        new_p[6].append(jnp.concatenate([st_g[last], st_v[last]], axis=-1))

        (o_gm, pool_in, fq, sq, vn, _, lft, _, fk, fv, sk, sv, _, _, _, _) = _inproj(
            xs, *common, gm_tile(gm_ws[i]), bs_full_sample(gm_bs[i], ts, nbs),
            layer=i, tb=ns_tok, seg=ts, gm_rows=ns_tok, gm_chunk=ts, nseq=1)
        grp_rows = -(-(POOL_BUF + ts) // SUBLANES) * SUBLANES
        lead = grp_rows - POOL_BUF - ts
        full = jnp.concatenate([jnp.zeros((nbs, lead, BR_WIDTH), F32), state_pool[i],
                                pool_in.reshape(nbs, ts, BR_WIDTH)], axis=1)
        o_pool = _pool(full.reshape(nbs * grp_rows, BR_WIDTH), w_pm_bd, row(pm_scale[i]),
                       tb=nbs * grp_rows, rows_per_seq=grp_rows, offset=grp_rows - ts, pos0=past)
        o_pool = o_pool.reshape(nbs, grp_rows, BR_WIDTH)[:, grp_rows - ts:].reshape(ns_tok, BR_WIDTH)
        o_fox, o_sb = _dec_attn(
            page_table, _block_diag_queries(fq, nbs, ts), _block_diag_queries(sq, nbs, ts),
            _pad_new(fk, nbs, ts), _pad_new(fv, nbs, ts),
            jnp.pad(jnp.swapaxes(lft.reshape(SUBLANES, nbs, ts), 0, 1), ((0, 0), (0, 0), (0, LANES - ts))),
            _pad_new(sk, nbs, ts), _pad_new(sv, nbs, ts),
            ck_f, cv_f, clf, ck_s, cv_s, layer=i, npp=npp, tnew=ts)
        o_fox = o_fox.reshape(ns_tok, BR_WIDTH).astype(BF)
        o_sb = o_sb.reshape(ns_tok, BR_WIDTH).astype(BF)
        x1 = _merge(xs, o_gm, o_pool, o_fox, o_sb, row(g_mix[i]), w_gate_all, w_br_b, w_out_b, layer=i, tb=ns_tok)
        buf = state_ffn_conv[i]
        zero = jnp.zeros((nbs, ts - 1, f2), F32)
        p1 = jnp.concatenate([buf[:, 1:2], zero], axis=1).reshape(ns_tok, f2)
        p2 = jnp.concatenate([buf, zero[:, 1:]], axis=1).reshape(ns_tok, f2)
        xs, a_g, a_v = _ffn(x1, p_sample[i].reshape(ns_tok, -1), row(g_ffn[i]), w_up_b, conv_w[i], row(conv_b[i]),
                            w_down_b, row(g_ple[i]), w_pg, w_pp, row(g_final), (p1, p2),
                            tb=ns_tok, seg=ts, tf=tf, final=final)
        new_s[0].append(heads(fk).reshape(nbs, ts, N_HEADS, HEAD_DIM))
        new_s[1].append(heads(fv).reshape(nbs, ts, N_HEADS, HEAD_DIM))
        new_s[2].append(logf(lft).reshape(nbs, ts, N_HEADS))
        new_s[3].append(heads(sk).reshape(nbs, ts, N_HEADS, HEAD_DIM))
        new_s[4].append(heads(sv).reshape(nbs, ts, N_HEADS, HEAD_DIM))
        new_s[5].append(jnp.concatenate([state_pool[i], pool_in.reshape(nbs, ts, BR_WIDTH)], axis=1)[:, ts:])
        a_full = jnp.concatenate([a_g, a_v], axis=-1).reshape(nbs, ts, f2)
        new_s[6].append(jnp.concatenate([buf, a_full], axis=1)[:, ts:])
        new_s[7].append(vn.reshape(nbs, ts, BR_WIDTH))

    stk = lambda lst: jnp.stack(lst, axis=0)
    return (xp.reshape(nbp, t, d), xs.reshape(nbs, ts, d),
            *[stk(a) for a in new_p], *[stk(a) for a in new_s])


def bs_full_sample(gm_bs_i, ts, nbs):
    return jnp.repeat(jnp.tile(gm_bs_i[:, :ts].T, (nbs, 1)), HEAD_DIM, axis=1)


def _block_diag_queries(q, nbs, ts):
    q = q.reshape(nbs, ts, 1, N_HEADS, HEAD_DIM)
    eye = jnp.eye(SUBLANES, N_HEADS, dtype=q.dtype).reshape(1, 1, SUBLANES, N_HEADS, 1)
    return (q * eye).reshape(nbs, ts * SUBLANES, BR_WIDTH)


def _pad_new(a, nbs, ts):
    return jnp.pad(jnp.swapaxes(a.reshape(BR_WIDTH, nbs, ts), 0, 1), ((0, 0), (0, 0), (0, LANES - ts)))
```
